```python
import jax, jax.numpy as jnp
from jax import lax
import numpy as np

D_MODEL = 1024
BATCH = 8
SEQ = 2048
DEPTH = 2
DEC_BATCH = 128
DEC_SEQ = 4
PAST_LEN = 2048
PAGE_SIZE = 128

RET_HEADS = 4
RET_DK = 128
RET_DV = 128
RET_QK = RET_HEADS * RET_DK
RET_V = RET_HEADS * RET_DV
RET_CHUNK = 128
ROPE_BASE = 10000.0
SB_HEADS = 4
SB_DH = 128
SB_W = SB_HEADS * SB_DH
SB_BLOCK = 128
SB_BIAS_INIT = -8.0
LRU_WIDTH = 512
LRU_BLOCKS = 4
LRU_BW = LRU_WIDTH // LRU_BLOCKS
LRU_CONV = 4
LRU_C = 8.0
N_BRANCH = 3
D_FF = 2816
FFN_CONV = 3
EPS = 1e-6

IN_SIZES = (RET_QK, RET_QK, RET_V, RET_V, SB_W, SB_W, SB_W, LRU_WIDTH, N_BRANCH * D_MODEL)
D_IN = int(sum(IN_SIZES))
IN_OFFSETS = tuple(int(o) for o in np.cumsum(IN_SIZES)[:-1])

kernel_name = "hybrid_ret_sb_rglru_step"

F32 = jnp.float32


def rmsnorm(x, g):
    xf = x.astype(F32)
    y = xf * lax.rsqrt(jnp.mean(xf * xf, axis=-1, keepdims=True) + EPS)
    return (y * g.astype(F32)).astype(x.dtype)


def rotary(x, pos):
    half = x.shape[-1] // 2
    inv = ROPE_BASE ** (-jnp.arange(half, dtype=F32) / half)
    ang = pos[:, None] * inv[None, :]
    cos = jnp.cos(ang)[None, :, None, :]
    sin = jnp.sin(ang)[None, :, None, :]
    x1, x2 = x[..., :half], x[..., half:]
    return jnp.concatenate([x1 * cos - x2 * sin, x2 * cos + x1 * sin], axis=-1)


def retention_log_decay():
    return jnp.log(1.0 - 2.0 ** (-5.0 - jnp.arange(RET_HEADS, dtype=F32)))


def retention_chunk(q, k, v, S, lg):
    L = q.shape[1]
    idx = jnp.arange(L, dtype=F32)
    diff = idx[:, None] - idx[None, :]
    decay = jnp.where(diff >= 0, jnp.exp(jnp.maximum(diff, 0.0)[None] * lg[:, None, None]), 0.0)
    scores = jnp.einsum('bihd,bjhd->bhij', q, k) * decay[None]
    intra = jnp.einsum('bhij,bjhe->bihe', scores, v)
    q_dec = jnp.exp((idx + 1.0)[:, None] * lg[None, :])
    cross = jnp.einsum('bihd,bhde->bihe', q, S) * q_dec[None, :, :, None]
    k_dec = jnp.exp((L - 1.0 - idx)[:, None] * lg[None, :])
    S_new = jnp.exp(L * lg)[None, :, None, None] * S + jnp.einsum('bjhd,bjhe->bhde', k * k_dec[None, :, :, None], v)
    return intra + cross, S_new


def retention_prompt(q, k, v, lg):
    B, T, H, dk = q.shape
    dv = v.shape[-1]
    nc = T // RET_CHUNK

    def to_chunks(a):
        return a.reshape(B, nc, RET_CHUNK, H, a.shape[-1]).transpose(1, 0, 2, 3, 4)

    def step(S, qkv):
        qc, kc, vc = qkv
        o, S = retention_chunk(qc, kc, vc, S, lg)
        return S, o

    S0 = jnp.zeros((B, H, dk, dv), F32)
    S, o = lax.scan(step, S0, (to_chunks(q), to_chunks(k), to_chunks(v)))
    return o.transpose(1, 0, 2, 3, 4).reshape(B, T, H, dv), S


def stick_breaking(q, k, v, t_idx, bias):
    z = (jnp.einsum('bthd,bshd->bhts', q.astype(F32), k.astype(F32)) * (SB_DH ** -0.5)
         + bias.astype(F32)[None, :, None, None])
    s_idx = jnp.arange(k.shape[1])
    mask = (s_idx[None, :] < t_idx[:, None])[None, None]
    log_keep = jnp.where(mask, jax.nn.log_sigmoid(-z), 0.0)
    later = lax.cumsum(log_keep, axis=3, reverse=True) - log_keep
    A = jnp.where(mask, jnp.exp(jax.nn.log_sigmoid(z) + later), 0.0)
    return jnp.einsum('bhts,bshd->bthd', A, v.astype(F32))


def causal_dwconv(x, buf, w, b):
    W = w.shape[0]
    T = x.shape[1]
    xp = jnp.concatenate([buf.astype(x.dtype), x], axis=1)
    out = b.astype(x.dtype) + xp[:, 0:T] * w[0]
    for i in range(1, W):
        out = out + xp[:, i:i + T] * w[i]
    return out, xp[:, -(W - 1):]


def rg_lru(x, h0, w_a, b_a, w_x, b_x, lam):
    B, T, _ = x.shape
    xf = x.astype(F32)
    xb = xf.reshape(B, T, LRU_BLOCKS, LRU_BW)
    r = jax.nn.sigmoid(jnp.einsum('btni,nij->btnj', xb, w_a.astype(F32)).reshape(B, T, LRU_WIDTH) + b_a.astype(F32))
    i = jax.nn.sigmoid(jnp.einsum('btni,nij->btnj', xb, w_x.astype(F32)).reshape(B, T, LRU_WIDTH) + b_x.astype(F32))
    log_a = LRU_C * r * jax.nn.log_sigmoid(lam.astype(F32))
    a = jnp.exp(log_a)
    bterm = jnp.sqrt(-jnp.expm1(2.0 * log_a)) * (i * xf)
    bterm = bterm.at[:, 0].add(a[:, 0] * h0.astype(F32))

    def combine(e1, e2):
        a1, b1 = e1
        a2, b2 = e2
        return a1 * a2, a2 * b1 + b2

    _, h = lax.associative_scan(combine, (a, bterm), axis=1)
    return h, h[:, -1]


def hybrid_layer(x, pos, lw, ret_S, lru_h, lru_buf, ffn_buf, past_k, past_v):
    B, T, _ = x.shape
    dt = x.dtype
    xn = rmsnorm(x, lw["norm1"])
    proj = xn @ lw["w_in"]
    rq, rk, rv, rg, sq, sk, sv, lx, gl = jnp.split(proj, IN_OFFSETS, axis=-1)

    lg = retention_log_decay()
    q = rotary(rq.astype(F32).reshape(B, T, RET_HEADS, RET_DK), pos)
    k = rotary(rk.astype(F32).reshape(B, T, RET_HEADS, RET_DK), pos) * (RET_DK ** -0.5)
    v = rv.astype(F32).reshape(B, T, RET_HEADS, RET_DV)
    if ret_S is None:
        o, S_new = retention_prompt(q, k, v, lg)
    else:
        o, S_new = retention_chunk(q, k, v, ret_S.astype(F32), lg)
    mu = jnp.mean(o, axis=-1, keepdims=True)
    var = jnp.mean(jnp.square(o - mu), axis=-1, keepdims=True)
    o = (o - mu) * lax.rsqrt(var + EPS)
    o_ret = (o.reshape(B, T, RET_V) * lw["ret_gn"].astype(F32) * jax.nn.silu(rg.astype(F32))).astype(dt)

    q_sb = sq.reshape(B, T, SB_HEADS, SB_DH)
    k_sb = sk.reshape(B, T, SB_HEADS, SB_DH)
    v_sb = sv.reshape(B, T, SB_HEADS, SB_DH)
    if past_k is None:
        blocks = []
        for bi in range(T // SB_BLOCK):
            lo, hi = bi * SB_BLOCK, (bi + 1) * SB_BLOCK
            blocks.append(stick_breaking(q_sb[:, lo:hi], k_sb[:, :hi], v_sb[:, :hi], jnp.arange(lo, hi), lw["sb_bias"]))
        o_sb = jnp.concatenate(blocks, axis=1)
    else:
        P = past_k.shape[1]
        k_all = jnp.concatenate([past_k.astype(dt), k_sb], axis=1)
        v_all = jnp.concatenate([past_v.astype(dt), v_sb], axis=1)
        o_sb = stick_breaking(q_sb, k_all, v_all, P + jnp.arange(T), lw["sb_bias"])
    o_sb = o_sb.reshape(B, T, SB_W).astype(dt)

    if lru_buf is None:
        lru_buf = jnp.zeros((B, LRU_CONV - 1, LRU_WIDTH), dt)
        lru_h = jnp.zeros((B, LRU_WIDTH), F32)
    xc, lru_buf_new = causal_dwconv(lx, lru_buf, lw["lru_conv_w"], lw["lru_conv_b"])
    h, h_last = rg_lru(xc, lru_h, lw["lru_w_a"], lw["lru_b_a"], lw["lru_w_x"], lw["lru_b_x"], lw["lru_lambda"])
    o_lru = h.astype(dt)

    g_all = jax.nn.sigmoid(gl.astype(F32))
    g_ret, g_sb, g_lru = jnp.split(g_all, N_BRANCH, axis=-1)
    merged = (g_ret * (o_ret @ lw["w_br_ret"]).astype(F32)
              + g_sb * (o_sb @ lw["w_br_sb"]).astype(F32)
              + g_lru * (o_lru @ lw["w_br_lru"]).astype(F32))
    x = x + merged.astype(dt) @ lw["w_out"]

    xn2 = rmsnorm(x, lw["norm2"])
    g = xn2 @ lw["w_ffn_gate"]
    u = xn2 @ lw["w_ffn_up"]
    if ffn_buf is None:
        ffn_buf = jnp.zeros((B, FFN_CONV - 1, D_FF), dt)
    gc, ffn_buf_new = causal_dwconv(g, ffn_buf, lw["ffn_conv_w"], lw["ffn_conv_b"])
    x = x + (jax.nn.gelu(gc) * u) @ lw["w_ffn_down"]
    return x, (k_sb, v_sb, S_new, h_last, lru_buf_new, ffn_buf_new)


def setup_inputs(seed: int = 0) -> dict:
    key = jax.random.key(seed)
    ks = iter(jax.random.split(key, 40))
    n_pages = PAST_LEN // PAGE_SIZE
    n_used = DEC_BATCH * n_pages
    n_phys = (n_used * 5) // 4

    def nrm(shape, scale):
        return jax.random.normal(next(ks), shape, F32) * scale

    x_prompt = nrm((BATCH, SEQ, D_MODEL), 1.0)
    x_sample = nrm((DEC_BATCH, DEC_SEQ, D_MODEL), 1.0)
    cache_sb_k = nrm((DEPTH, n_phys, PAGE_SIZE, SB_HEADS, SB_DH), 1.0)
    cache_sb_v = nrm((DEPTH, n_phys, PAGE_SIZE, SB_HEADS, SB_DH), 1.0)
    state_ret = nrm((DEPTH, DEC_BATCH, RET_HEADS, RET_DK, RET_DV), 0.1)
    state_lru_h = nrm((DEPTH, DEC_BATCH, LRU_WIDTH), 0.5)
    state_lru_conv = nrm((DEPTH, DEC_BATCH, LRU_CONV - 1, LRU_WIDTH), 1.0)
    state_ffn_conv = nrm((DEPTH, DEC_BATCH, FFN_CONV - 1, D_FF), 1.0)
    page_table = jax.random.permutation(next(ks), n_phys)[:n_used].reshape(DEC_BATCH, n_pages).astype(jnp.int32)

    norm1 = 1.0 + nrm((DEPTH, D_MODEL), 0.01)
    w_in = nrm((DEPTH, D_MODEL, D_IN), D_MODEL ** -0.5)
    ret_gn = 1.0 + nrm((DEPTH, RET_V), 0.01)
    sb_bias = SB_BIAS_INIT + nrm((DEPTH, SB_HEADS), 0.1)
    lru_conv_w = nrm((DEPTH, LRU_CONV, LRU_WIDTH), LRU_CONV ** -0.5)
    lru_conv_b = nrm((DEPTH, LRU_WIDTH), 0.01)
    lru_w_a = nrm((DEPTH, LRU_BLOCKS, LRU_BW, LRU_BW), LRU_BW ** -0.5)
    lru_b_a = nrm((DEPTH, LRU_WIDTH), 0.01)
    lru_w_x = nrm((DEPTH, LRU_BLOCKS, LRU_BW, LRU_BW), LRU_BW ** -0.5)
    lru_b_x = nrm((DEPTH, LRU_WIDTH), 0.01)
    a_target = jax.random.uniform(next(ks), (DEPTH, LRU_WIDTH), F32, 0.9, 0.999) ** (1.0 / LRU_C)
    lru_lambda = jnp.log(a_target) - jnp.log1p(-a_target)
    w_br_ret = nrm((DEPTH, RET_V, D_MODEL), RET_V ** -0.5)
    w_br_sb = nrm((DEPTH, SB_W, D_MODEL), SB_W ** -0.5)
    w_br_lru = nrm((DEPTH, LRU_WIDTH, D_MODEL), LRU_WIDTH ** -0.5)
    w_out = nrm((DEPTH, D_MODEL, D_MODEL), D_MODEL ** -0.5)
    norm2 = 1.0 + nrm((DEPTH, D_MODEL), 0.01)
    w_ffn_gate = nrm((DEPTH, D_MODEL, D_FF), D_MODEL ** -0.5)
    w_ffn_up = nrm((DEPTH, D_MODEL, D_FF), D_MODEL ** -0.5)
    ffn_conv_w = nrm((DEPTH, FFN_CONV, D_FF), FFN_CONV ** -0.5)
    ffn_conv_b = nrm((DEPTH, D_FF), 0.01)
    w_ffn_down = nrm((DEPTH, D_FF, D_MODEL), D_FF ** -0.5)
    norm_f = 1.0 + nrm((D_MODEL,), 0.01)
    return {
        "x_prompt": x_prompt, "x_sample": x_sample,
        "cache_sb_k": cache_sb_k, "cache_sb_v": cache_sb_v,
        "state_ret": state_ret, "state_lru_h": state_lru_h,
        "state_lru_conv": state_lru_conv, "state_ffn_conv": state_ffn_conv,
        "page_table": page_table,
        "norm1": norm1, "w_in": w_in, "ret_gn": ret_gn, "sb_bias": sb_bias,
        "lru_conv_w": lru_conv_w, "lru_conv_b": lru_conv_b,
        "lru_w_a": lru_w_a, "lru_b_a": lru_b_a, "lru_w_x": lru_w_x, "lru_b_x": lru_b_x,
        "lru_lambda": lru_lambda,
        "w_br_ret": w_br_ret, "w_br_sb": w_br_sb, "w_br_lru": w_br_lru, "w_out": w_out,
        "norm2": norm2, "w_ffn_gate": w_ffn_gate, "w_ffn_up": w_ffn_up,
        "ffn_conv_w": ffn_conv_w, "ffn_conv_b": ffn_conv_b, "w_ffn_down": w_ffn_down,
        "norm_f": norm_f,
    }


def reference(x_prompt, x_sample, cache_sb_k, cache_sb_v, state_ret, state_lru_h, state_lru_conv,
              state_ffn_conv, page_table, norm1, w_in, ret_gn, sb_bias, lru_conv_w, lru_conv_b, lru_w_a, lru_b_a,
              lru_w_x, lru_b_x, lru_lambda, w_br_ret, w_br_sb, w_br_lru, w_out, norm2, w_ffn_gate,
              w_ffn_up, ffn_conv_w, ffn_conv_b, w_ffn_down, norm_f):
    def layer_weights(l):
        return {
            "norm1": norm1[l], "w_in": w_in[l], "ret_gn": ret_gn[l], "sb_bias": sb_bias[l],
            "lru_conv_w": lru_conv_w[l], "lru_conv_b": lru_conv_b[l],
            "lru_w_a": lru_w_a[l], "lru_b_a": lru_b_a[l], "lru_w_x": lru_w_x[l], "lru_b_x": lru_b_x[l],
            "lru_lambda": lru_lambda[l],
            "w_br_ret": w_br_ret[l], "w_br_sb": w_br_sb[l], "w_br_lru": w_br_lru[l], "w_out": w_out[l],
            "norm2": norm2[l], "w_ffn_gate": w_ffn_gate[l], "w_ffn_up": w_ffn_up[l],
            "ffn_conv_w": ffn_conv_w[l], "ffn_conv_b": ffn_conv_b[l], "w_ffn_down": w_ffn_down[l],
        }

    T_p = x_prompt.shape[1]
    pos_p = jnp.arange(T_p, dtype=F32)
    xp = x_prompt
    st_p = []
    for l in range(DEPTH):
        xp, st = hybrid_layer(xp, pos_p, layer_weights(l), None, None, None, None, None, None)
        st_p.append(st)
    y_prompt = rmsnorm(xp, norm_f)

    DB, T_s, _ = x_sample.shape
    past_len = page_table.shape[1] * cache_sb_k.shape[2]
    pos_s = past_len + jnp.arange(T_s, dtype=F32)
    xs = x_sample
    st_s = []
    for l in range(DEPTH):
        past_k = cache_sb_k[l][page_table].reshape(DB, past_len, SB_HEADS, SB_DH)
        past_v = cache_sb_v[l][page_table].reshape(DB, past_len, SB_HEADS, SB_DH)
        xs, st = hybrid_layer(xs, pos_s, layer_weights(l), state_ret[l], state_lru_h[l], state_lru_conv[l],
                              state_ffn_conv[l], past_k, past_v)
        st_s.append(st)
    y_sample = rmsnorm(xs, norm_f)

    def stack(sts, j):
        return jnp.stack([s[j] for s in sts], axis=0)

    return (y_prompt, y_sample,
            stack(st_p, 0), stack(st_p, 1), stack(st_s, 0), stack(st_s, 1),
            stack(st_p, 2), stack(st_s, 2),
            stack(st_p, 3), stack(st_s, 3),
            stack(st_p, 4), stack(st_s, 4),
            stack(st_p, 5), stack(st_s, 5))
```

```python
import functools

import jax
import jax.numpy as jnp
from jax import lax
from jax.experimental import pallas as pl
from jax.experimental.pallas import tpu as pltpu

F32 = jnp.float32
BF16 = jnp.bfloat16

D_MODEL = 1024
RET_HEADS = 4
RET_DK = 128
RET_DV = 128
RET_CHUNK = 128
ROPE_BASE = 10000.0
SB_HEADS = 4
SB_DH = 128
SB_BLOCK = 128
LRU_WIDTH = 512
LRU_BLOCKS = 4
LRU_BW = LRU_WIDTH // LRU_BLOCKS
LRU_CONV = 4
LRU_C = 8.0
N_BRANCH = 3
D_FF = 2816
FFN_CONV = 3
EPS = 1e-6

RET_W = RET_HEADS * RET_DK
SB_W = SB_HEADS * SB_DH
PROJ_COLS = 4 * RET_W + 3 * SB_W + LRU_WIDTH
GATE_COLS = N_BRANCH * D_MODEL

V7X_LANES = 128
V7X_SUBLANES = 8
V7X_VMEM_LIMIT_BYTES = 60000 * 1024

TOKEN_TILE = 512
FFN_COL_CHUNK = 256
LRU_ROW_CHUNK = 256


def _cparams(*sem):
    return pltpu.CompilerParams(dimension_semantics=sem, vmem_limit_bytes=V7X_VMEM_LIMIT_BYTES)


def _resident(shape):
    zeros = (0,) * len(shape)
    return pl.BlockSpec(shape, lambda *_: zeros, pipeline_mode=pl.Buffered(1))


def _mm(a, b):
    return jnp.dot(a, b, preferred_element_type=F32)


def _mm_nt(a, b):
    return lax.dot_general(a, b, (((1,), (1,)), ((), ())), preferred_element_type=F32)


def _mm_tn(a, b):
    return lax.dot_general(a, b, (((0,), (0,)), ((), ())), preferred_element_type=F32)


def _rms(x, g):
    return x * lax.rsqrt(jnp.mean(x * x, axis=-1, keepdims=True) + EPS) * g


def _softplus(z):
    return jnp.maximum(z, 0.0) + jnp.log1p(jnp.exp(-jnp.abs(z)))


def _suffix_sum(lk, tri):
    hi = lk.astype(BF16)
    lo = (lk - hi.astype(F32)).astype(BF16)
    return _mm(hi, tri) + _mm(lo, tri)


def _strict_lower(n):
    r = lax.broadcasted_iota(jnp.int32, (n, n), 0)
    c = lax.broadcasted_iota(jnp.int32, (n, n), 1)
    return (r > c).astype(BF16)


def _proj_kernel(x_ref, g_ref, w_ref, ret_ref, sbq_ref, sbk_ref, sbv_ref, lx_ref):
    xn = _rms(x_ref[...], g_ref[...]).astype(BF16)
    for c in range(4):
        ret_ref[:, c * RET_W:(c + 1) * RET_W] = _mm(xn, w_ref[:, c * RET_W:(c + 1) * RET_W])
    o = 4 * RET_W
    sbq_ref[...] = _mm(xn, w_ref[:, o:o + SB_W]).astype(BF16)
    sbk_ref[...] = _mm(xn, w_ref[:, o + SB_W:o + 2 * SB_W])
    sbv_ref[...] = _mm(xn, w_ref[:, o + 2 * SB_W:o + 3 * SB_W])
    lx_ref[...] = _mm(xn, w_ref[:, o + 3 * SB_W:o + 3 * SB_W + LRU_WIDTH])


def _proj(x, g, w):
    n = x.shape[0]
    tm = TOKEN_TILE
    row = lambda width: pl.BlockSpec((tm, width), lambda i: (i, 0))
    return pl.pallas_call(
        _proj_kernel,
        grid=(n // tm,),
        in_specs=[row(D_MODEL), _resident((1, D_MODEL)), _resident((D_MODEL, PROJ_COLS))],
        out_specs=[row(4 * RET_W), row(SB_W), row(SB_W), row(SB_W), row(LRU_WIDTH)],
        out_shape=[
            jax.ShapeDtypeStruct((n, 4 * RET_W), F32),
            jax.ShapeDtypeStruct((n, SB_W), BF16),
            jax.ShapeDtypeStruct((n, SB_W), F32),
            jax.ShapeDtypeStruct((n, SB_W), F32),
            jax.ShapeDtypeStruct((n, LRU_WIDTH), F32),
        ],
        compiler_params=_cparams("parallel"),
        name="proj",
    )(x, g, w)


def _rotary(x, cs, sn):
    return x * cs + pltpu.roll(x, RET_DK // 2, 1) * sn


def _group_norm_gate(o, gn, gate):
    mu = jnp.mean(o, axis=-1, keepdims=True)
    d = o - mu
    var = jnp.mean(d * d, axis=-1, keepdims=True)
    return d * lax.rsqrt(var + EPS) * gn * (gate * jax.nn.sigmoid(gate))


def _ret_prompt_kernel(lg_ref, q_ref, k_ref, v_ref, g_ref, cos_ref, sin_ref, gn_ref, o_ref, s_ref):
    lg = lg_ref[pl.program_id(1)]
    L = RET_CHUNK
    n_chunks = q_ref.shape[0] // L
    row = lax.broadcasted_iota(jnp.int32, (L, L), 0).astype(F32)
    col = lax.broadcasted_iota(jnp.int32, (L, L), 1).astype(F32)
    diff = row - col
    dmat = jnp.where(diff >= 0, jnp.exp(jnp.maximum(diff, 0.0) * lg), 0.0)
    q_dec = jnp.exp((row + 1.0) * lg)
    k_dec = jnp.exp((L - 1.0 - row) * lg)
    s_dec = jnp.exp(jnp.full((RET_DK, RET_DV), float(L), F32) * lg)
    gn = gn_ref[...]
    scale = RET_DK ** -0.5

    def chunk(c, s):
        rows = pl.ds(pl.multiple_of(c * L, L), L)
        cs, sn = cos_ref[rows, :], sin_ref[rows, :]
        q = _rotary(q_ref[rows, :], cs, sn)
        k = _rotary(k_ref[rows, :], cs, sn) * scale
        qb, kb, vb = q.astype(BF16), k.astype(BF16), v_ref[rows, :].astype(BF16)
        scores = _mm_nt(qb, kb) * dmat
        o = _mm(scores.astype(BF16), vb) + _mm(qb, s.astype(BF16)) * q_dec
        o_ref[rows, :] = _group_norm_gate(o, gn, g_ref[rows, :]).astype(BF16)
        return s_dec * s + _mm_tn((k * k_dec).astype(BF16), vb)

    s_ref[0, 0] = lax.fori_loop(0, n_chunks, chunk, jnp.zeros((RET_DK, RET_DV), F32))


def _ret_prompt(ret, lg, cos, sin, gn, batch, seq):
    H = RET_HEADS
    col = lambda j: pl.BlockSpec((seq, RET_DK), lambda b, h, j=j: (b, j * H + h))
    return pl.pallas_call(
        _ret_prompt_kernel,
        grid=(batch, H),
        in_specs=[
            pl.BlockSpec(memory_space=pltpu.SMEM),
            col(0), col(1), col(2), col(3),
            _resident((seq, RET_DK)), _resident((seq, RET_DK)),
            pl.BlockSpec((1, RET_DV), lambda b, h: (0, h)),
        ],
        out_specs=[
            pl.BlockSpec((seq, RET_DV), lambda b, h: (b, h)),
            pl.BlockSpec((1, 1, RET_DK, RET_DV), lambda b, h: (b, h, 0, 0)),
        ],
        out_shape=[
            jax.ShapeDtypeStruct((batch * seq, RET_W), BF16),
            jax.ShapeDtypeStruct((batch, H, RET_DK, RET_DV), F32),
        ],
        compiler_params=_cparams("parallel", "parallel"),
        name="ret_prompt",
    )(lg, ret, ret, ret, ret, cos, sin, gn)


def _ret_sample_kernel(lg_ref, q_ref, k_ref, v_ref, g_ref, cos_ref, sin_ref, gn_ref, st_ref, o_ref, so_ref,
                       *, nb, steps):
    R = nb * steps
    P = RET_DK
    shift = steps.bit_length() - 1
    scale = RET_DK ** -0.5
    ri = lax.broadcasted_iota(jnp.int32, (R, P), 0)
    ci = lax.broadcasted_iota(jnp.int32, (R, P), 1)
    rb, rt = ri >> shift, ri & (steps - 1)
    cb, ct = ci >> shift, ci & (steps - 1)
    same = (rb == cb) & (ci < R) & (ct <= rt)
    dt = jnp.maximum(rt - ct, 0).astype(F32)
    t_row = rt.astype(F32)
    wide_r = lax.broadcasted_iota(jnp.int32, (R, nb * RET_DV), 0) >> shift
    wide_c = lax.broadcasted_iota(jnp.int32, (R, nb * RET_DV), 1) >> (RET_DV.bit_length() - 1)
    own = wide_r == wide_c
    zpad = jnp.zeros((P - R, RET_DK), BF16)
    zpad_w = jnp.zeros((P - R, nb * RET_DV), BF16)
    cs, sn = cos_ref[...], sin_ref[...]
    for h in range(RET_HEADS):
        lg = lg_ref[h]
        sl = slice(h * RET_DK, (h + 1) * RET_DK)
        dmat = jnp.where(same, jnp.exp(dt * lg), 0.0)
        q_dec = jnp.exp((t_row + 1.0) * lg)
        k_dec = jnp.exp((steps - 1.0 - t_row) * lg)
        s_dec = jnp.exp(jnp.full((RET_DK, RET_DV), float(steps), F32) * lg)
        q = _rotary(q_ref[:, sl], cs, sn)
        k = _rotary(k_ref[:, sl], cs, sn) * scale
        v = v_ref[:, sl]
        qb = q.astype(BF16)
        k_pad = jnp.concatenate([k.astype(BF16), zpad], axis=0)
        v_pad = jnp.concatenate([v.astype(BF16), zpad], axis=0)
        scores = _mm_nt(qb, k_pad) * dmat
        intra = _mm(scores.astype(BF16), v_pad)
        s_cat = jnp.concatenate([st_ref[b, h] for b in range(nb)], axis=1)
        wide = jnp.where(own, _mm(qb, s_cat.astype(BF16)), 0.0)
        cross = wide[:, 0:RET_DV]
        for b in range(1, nb):
            cross = cross + wide[:, b * RET_DV:(b + 1) * RET_DV]
        o = intra + cross * q_dec
        o_ref[:, sl] = _group_norm_gate(o, gn_ref[:, sl], g_ref[:, sl]).astype(BF16)
        kd_pad = jnp.concatenate([(k * k_dec).astype(BF16), zpad], axis=0)
        v_wide = jnp.where(own, jnp.concatenate([v] * nb, axis=1), 0.0).astype(BF16)
        upd = _mm_tn(kd_pad, jnp.concatenate([v_wide, zpad_w], axis=0))
        for b in range(nb):
            so_ref[b, h] = s_dec * st_ref[b, h] + upd[:, b * RET_DV:(b + 1) * RET_DV]


def _ret_sample(ret_tok, lg, cos, sin, gn, state, steps):
    n = ret_tok.shape[0]
    batch = n // steps
    nb = 8
    R = nb * steps
    col = lambda j: pl.BlockSpec((R, RET_W), lambda i, j=j: (i, j))
    st_spec = pl.BlockSpec((nb, RET_HEADS, RET_DK, RET_DV), lambda i: (i, 0, 0, 0))
    return pl.pallas_call(
        functools.partial(_ret_sample_kernel, nb=nb, steps=steps),
        grid=(batch // nb,),
        in_specs=[
            pl.BlockSpec(memory_space=pltpu.SMEM),
            col(0), col(1), col(2), col(3),
            _resident((R, RET_DK)), _resident((R, RET_DK)), _resident((1, RET_W)),
            st_spec,
        ],
        out_specs=[pl.BlockSpec((R, RET_W), lambda i: (i, 0)), st_spec],
        out_shape=[
            jax.ShapeDtypeStruct((n, RET_W), BF16),
            jax.ShapeDtypeStruct(state.shape, F32),
        ],
        compiler_params=_cparams("parallel"),
        name="ret_sample",
    )(lg, ret_tok, ret_tok, ret_tok, ret_tok, cos, sin, gn, state)


def _sb_prompt_kernel(bias_ref, q_ref, k_ref, v_ref, o_ref, kb_ref, vb_ref):
    bias = bias_ref[pl.program_id(1)]
    n = SB_BLOCK
    n_blocks = q_ref.shape[0] // n
    scale = SB_DH ** -0.5
    kb_ref[...] = k_ref[...].astype(BF16)
    vb_ref[...] = v_ref[...].astype(BF16)
    tri = _strict_lower(n)
    row = lax.broadcasted_iota(jnp.int32, (n, n), 0)
    col = lax.broadcasted_iota(jnp.int32, (n, n), 1)
    causal = col < row

    def q_block(qi, _):
        rows = pl.ds(pl.multiple_of(qi * n, n), n)
        q = q_ref[rows, :]
        z = _mm_nt(q, kb_ref[rows, :]) * scale + bias
        sp = _softplus(z)
        lk = jnp.where(causal, -sp, 0.0)
        a = jnp.where(causal, jnp.exp(z - sp + _suffix_sum(lk, tri)), 0.0)
        acc = _mm(a.astype(BF16), vb_ref[rows, :])
        run = jnp.sum(lk, axis=1, keepdims=True)

        def k_block(j, carry):
            acc, run = carry
            krows = pl.ds(pl.multiple_of((qi - 1 - j) * n, n), n)
            z = _mm_nt(q, kb_ref[krows, :]) * scale + bias
            sp = _softplus(z)
            lk = -sp
            a = jnp.exp(z - sp + _suffix_sum(lk, tri) + run)
            acc = acc + _mm(a.astype(BF16), vb_ref[krows, :])
            return acc, run + jnp.sum(lk, axis=1, keepdims=True)

        acc, run = lax.fori_loop(0, qi, k_block, (acc, run))
        o_ref[rows, :] = acc.astype(BF16)
        return 0

    lax.fori_loop(0, n_blocks, q_block, 0)


def _sb_prompt(sbq, sbk, sbv, bias, batch, seq):
    spec = pl.BlockSpec((seq, SB_DH), lambda b, h: (b, h))
    return pl.pallas_call(
        _sb_prompt_kernel,
        grid=(batch, SB_HEADS),
        in_specs=[pl.BlockSpec(memory_space=pltpu.SMEM), spec, spec, spec],
        out_specs=spec,
        out_shape=jax.ShapeDtypeStruct((batch * seq, SB_W), BF16),
        scratch_shapes=[pltpu.VMEM((seq, SB_DH), BF16), pltpu.VMEM((seq, SB_DH), BF16)],
        compiler_params=_cparams("parallel", "parallel"),
        name="sb_prompt",
    )(bias, sbq, sbk, sbv)


def _sb_sample_kernel(pt_ref, bias_ref, q_ref, kn_ref, vn_ref, k_hbm, v_hbm, o_ref, kbuf, vbuf, sem,
                      *, layer, n_pages, steps):
    b = pl.program_id(0)
    slot = b % 2
    H = SB_HEADS
    hbits = H.bit_length() - 1
    nq = steps * H
    page_rows = kbuf.shape[2]
    scale = SB_DH ** -0.5

    def copies(bb, sl):
        out = []
        for p in range(n_pages):
            page = pt_ref[bb * n_pages + p]
            out.append(pltpu.make_async_copy(k_hbm.at[layer, page], kbuf.at[sl, p], sem.at[0, sl]))
            out.append(pltpu.make_async_copy(v_hbm.at[layer, page], vbuf.at[sl, p], sem.at[1, sl]))
        return out

    @pl.when(b == 0)
    def _():
        for c in copies(0, 0):
            c.start()

    @pl.when(b + 1 < pl.num_programs(0))
    def _():
        for c in copies(b + 1, 1 - slot):
            c.start()

    q = q_ref[0]

    def head_masks(width):
        ri = lax.broadcasted_iota(jnp.int32, (nq, width), 0)
        ci = lax.broadcasted_iota(jnp.int32, (nq, width), 1)
        rh = ri & (H - 1)
        bias = jnp.where(rh == 0, bias_ref[0], jnp.where(rh == 1, bias_ref[1],
                         jnp.where(rh == 2, bias_ref[2], bias_ref[3])))
        return ri, ci, rh == (ci & (H - 1)), bias

    _, _, same_head, bias_col = head_masks(page_rows)
    tri = _strict_lower(page_rows)

    P = V7X_LANES
    zpad = jnp.zeros((P - nq, SB_DH), BF16)
    kn = jnp.concatenate([kn_ref[0].astype(BF16), zpad], axis=0)
    vn = jnp.concatenate([vn_ref[0].astype(BF16), zpad], axis=0)
    rn, cn, same_new, bias_new = head_masks(P)
    valid = same_new & (cn < nq) & ((cn >> hbits) < (rn >> hbits))
    z = _mm_nt(q, kn) * scale + bias_new
    sp = _softplus(z)
    lk = jnp.where(valid, -sp, 0.0)
    a = jnp.where(valid, jnp.exp(z - sp + _suffix_sum(lk, _strict_lower(P))), 0.0)
    acc = _mm(a.astype(BF16), vn)
    run = jnp.sum(lk, axis=1, keepdims=True)

    for c in copies(b, slot):
        c.wait()

    def page(i, carry):
        acc, run = carry
        p = n_pages - 1 - i
        z = _mm_nt(q, kbuf[slot, p].astype(BF16)) * scale + bias_col
        sp = _softplus(z)
        lk = jnp.where(same_head, -sp, 0.0)
        a = jnp.where(same_head, jnp.exp(z - sp + _suffix_sum(lk, tri) + run), 0.0)
        acc = acc + _mm(a.astype(BF16), vbuf[slot, p].astype(BF16))
        return acc, run + jnp.sum(lk, axis=1, keepdims=True)

    acc, run = lax.fori_loop(0, n_pages, page, (acc, run))
    o_ref[0] = acc.astype(BF16)


def _sb_sample(page_table, bias, q16, kn16, vn16, cache_k, cache_v, layer, steps):
    batch, n_pages = page_table.shape
    nq = steps * SB_HEADS
    page_rows = cache_k.shape[2]
    row = pl.BlockSpec((1, nq, SB_DH), lambda b, pt: (b, 0, 0))
    return pl.pallas_call(
        functools.partial(_sb_sample_kernel, layer=layer, n_pages=n_pages, steps=steps),
        grid_spec=pltpu.PrefetchScalarGridSpec(
            num_scalar_prefetch=1,
            grid=(batch,),
            in_specs=[
                pl.BlockSpec(memory_space=pltpu.SMEM),
                row, row, row,
                pl.BlockSpec(memory_space=pl.ANY),
                pl.BlockSpec(memory_space=pl.ANY),
            ],
            out_specs=row,
            scratch_shapes=[
                pltpu.VMEM((2, n_pages, page_rows, SB_DH), F32),
                pltpu.VMEM((2, n_pages, page_rows, SB_DH), F32),
                pltpu.SemaphoreType.DMA((2, 2)),
            ],
        ),
        out_shape=jax.ShapeDtypeStruct((batch, nq, SB_DH), BF16),
        compiler_params=_cparams("arbitrary"),
        name="sb_sample",
    )(page_table.reshape(-1), bias, q16, kn16, vn16, cache_k, cache_v)


def _lru_kernel(x_ref, hist_ref, h0_ref, cw_ref, cb_ref, wa_ref, ba_ref, wx_ref, bx_ref, lam_ref,
                h_ref, hl_ref, tail, hc, a_s, b_s, *, sub):
    tc, rows, width = x_ref.shape
    taps = LRU_CONV

    @pl.when(pl.program_id(0) == 0)
    def _():
        tail[...] = hist_ref[...]
        hc[...] = h0_ref[...]

    log_lam = -_softplus(-lam_ref[...])
    for j in range(tc // sub):
        t0 = j * sub
        if t0 < taps - 1:
            xe = jnp.concatenate([tail[t0:taps - 1], x_ref[0:t0 + sub]], axis=0)
        else:
            xe = x_ref[t0 - (taps - 1):t0 + sub]
        xc = cb_ref[...] + xe[0:sub] * cw_ref[0]
        for i in range(1, taps):
            xc = xc + xe[i:i + sub] * cw_ref[i]
        xc = xc.reshape(sub * rows, width)
        r_parts, i_parts = [], []
        for n in range(LRU_BLOCKS):
            blk = xc[:, n * LRU_BW:(n + 1) * LRU_BW].astype(BF16)
            r_parts.append(_mm(blk, wa_ref[n]))
            i_parts.append(_mm(blk, wx_ref[n]))
        r = jax.nn.sigmoid(jnp.concatenate(r_parts, axis=1) + ba_ref[...])
        gate_i = jax.nn.sigmoid(jnp.concatenate(i_parts, axis=1) + bx_ref[...])
        log_a = LRU_C * r * log_lam
        a_s[t0:t0 + sub] = jnp.exp(log_a).reshape(sub, rows, width)
        one_minus_a2 = -jnp.tanh(log_a) * (jnp.exp(2.0 * log_a) + 1.0)
        b_s[t0:t0 + sub] = (jnp.sqrt(one_minus_a2) * (gate_i * xc)).reshape(sub, rows, width)
    tail[...] = x_ref[tc - (taps - 1):tc]

    def step(t, h):
        h = a_s[t] * h + b_s[t]
        h_ref[t] = h
        return h

    h = lax.fori_loop(0, tc, step, hc[...], unroll=min(tc, 8))
    hc[...] = h
    hl_ref[...] = h


def _lru(x_tm, hist, h0, cw, cb, wa, ba, wx, bx, lam, tc):
    steps, rows, width = x_tm.shape
    sub = max(1, LRU_ROW_CHUNK // rows)
    blk = pl.BlockSpec((tc, rows, width), lambda i: (i, 0, 0))
    vec = _resident((1, width))
    return pl.pallas_call(
        functools.partial(_lru_kernel, sub=sub),
        grid=(steps // tc,),
        in_specs=[
            blk, _resident((LRU_CONV - 1, rows, width)), _resident((rows, width)),
            _resident((LRU_CONV, 1, width)), vec,
            _resident((LRU_BLOCKS, LRU_BW, LRU_BW)), vec,
            _resident((LRU_BLOCKS, LRU_BW, LRU_BW)), vec, vec,
        ],
        out_specs=[blk, pl.BlockSpec((rows, width), lambda i: (0, 0))],
        out_shape=[
            jax.ShapeDtypeStruct((steps, rows, width), F32),
            jax.ShapeDtypeStruct((rows, width), F32),
        ],
        scratch_shapes=[
            pltpu.VMEM((LRU_CONV - 1, rows, width), F32),
            pltpu.VMEM((rows, width), F32),
            pltpu.VMEM((tc, rows, width), F32),
            pltpu.VMEM((tc, rows, width), F32),
        ],
        compiler_params=_cparams("arbitrary"),
        name="lru",
    )(x_tm, hist, h0, cw, cb, wa, ba, wx, bx, lam)


def _merge_kernel(x_ref, oret_ref, osb_ref, olru_ref, g_ref, wgl_ref, wr_ref, ws_ref, wl_ref, wo_ref,
                  out_ref, m_ref):
    x = x_ref[...]
    xn = _rms(x, g_ref[...]).astype(BF16)
    branches = ((oret_ref, wr_ref), (osb_ref, ws_ref), (olru_ref, wl_ref))
    cw = 256
    for c in range(D_MODEL // cw):
        cols = slice(c * cw, (c + 1) * cw)
        m = None
        for j, (o_ref, w_ref) in enumerate(branches):
            gate = jax.nn.sigmoid(_mm(xn, wgl_ref[:, j * D_MODEL + c * cw:j * D_MODEL + (c + 1) * cw]))
            term = gate * _mm(o_ref[...], w_ref[:, cols])
            m = term if m is None else m + term
        m_ref[:, cols] = m.astype(BF16)
    out_ref[...] = x + _mm(m_ref[...], wo_ref[...])


def _merge(x, o_ret, o_sb, o_lru, g, wgl, wr, ws, wl, wo):
    n = x.shape[0]
    tm = TOKEN_TILE
    row = lambda width: pl.BlockSpec((tm, width), lambda i: (i, 0))
    return pl.pallas_call(
        _merge_kernel,
        grid=(n // tm,),
        in_specs=[
            row(D_MODEL), row(RET_W), row(SB_W), row(LRU_WIDTH),
            _resident((1, D_MODEL)), _resident((D_MODEL, GATE_COLS)),
            _resident((RET_W, D_MODEL)), _resident((SB_W, D_MODEL)), _resident((LRU_WIDTH, D_MODEL)),
            _resident((D_MODEL, D_MODEL)),
        ],
        out_specs=row(D_MODEL),
        out_shape=jax.ShapeDtypeStruct((n, D_MODEL), F32),
        scratch_shapes=[pltpu.VMEM((tm, D_MODEL), BF16)],
        compiler_params=_cparams("parallel"),
        name="merge",
    )(x, o_ret, o_sb, o_lru, g, wgl, wr, ws, wl, wo)


def _ffn_kernel(x_ref, g_ref, wg_ref, wu_ref, cw_ref, cb_ref, wd_ref, hist_ref, nf_ref, out_ref, st_ref,
                carry, *, shift, tiles_per_seq, final_norm):
    tm = x_ref.shape[0]
    hist_rows = carry.shape[0]
    i = pl.program_id(0)

    @pl.when(i % tiles_per_seq == 0)
    def _():
        carry[...] = hist_ref[...]

    x = x_ref[...]
    xn = _rms(x, g_ref[...]).astype(BF16)
    acc = jnp.zeros((tm, D_MODEL), F32)
    S = V7X_SUBLANES
    for c in range(D_FF // FFN_COL_CHUNK):
        cols = slice(c * FFN_COL_CHUNK, (c + 1) * FFN_COL_CHUNK)
        g = _mm(xn, wg_ref[:, cols])
        u = _mm(xn, wu_ref[:, cols])
        hist = carry[:, cols]
        if shift == 1:
            ext = jnp.concatenate([hist, g[0:S]], axis=0)
            p1 = jnp.concatenate([pltpu.roll(ext, 1, 0)[S:2 * S], pltpu.roll(g, 1, 0)[S:]], axis=0)
            p2 = jnp.concatenate([pltpu.roll(ext, 2, 0)[S:2 * S], pltpu.roll(g, 2, 0)[S:]], axis=0)
        else:
            p1 = jnp.concatenate([hist[shift:2 * shift], g[0:tm - shift]], axis=0)
            p2 = jnp.concatenate([hist[0:2 * shift], g[0:tm - 2 * shift]], axis=0)
        carry[:, cols] = g[tm - hist_rows:tm]
        gc = cb_ref[:, cols] + p2 * cw_ref[0, :, cols] + p1 * cw_ref[1, :, cols] + g * cw_ref[2, :, cols]
        hcol = (jax.nn.gelu(gc) * u).astype(BF16)
        acc = acc + _mm(hcol, wd_ref[cols, :])
    st_ref[...] = carry[hist_rows - 2 * shift:hist_rows, :].reshape(st_ref.shape)
    y = x + acc
    if final_norm:
        y = _rms(y, nf_ref[...])
    out_ref[...] = y


def _ffn(x, g, wg, wu, cw, cb, wd, hist, nf, shift, tiles_per_seq, final_norm):
    n = x.shape[0]
    tm = TOKEN_TILE
    n_seq = n // (tm * tiles_per_seq)
    hist_rows = hist.shape[0]
    row = pl.BlockSpec((tm, D_MODEL), lambda i: (i, 0))
    return pl.pallas_call(
        functools.partial(_ffn_kernel, shift=shift, tiles_per_seq=tiles_per_seq, final_norm=final_norm),
        grid=(n // tm,),
        in_specs=[
            row, _resident((1, D_MODEL)),
            _resident((D_MODEL, D_FF)), _resident((D_MODEL, D_FF)),
            _resident((FFN_CONV, 1, D_FF)), _resident((1, D_FF)),
            _resident((D_FF, D_MODEL)),
            _resident((hist_rows, D_FF)), _resident((1, D_MODEL)),
        ],
        out_specs=[row, pl.BlockSpec((1, 2 * shift, D_FF), lambda i: (i // tiles_per_seq, 0, 0))],
        out_shape=[
            jax.ShapeDtypeStruct((n, D_MODEL), F32),
            jax.ShapeDtypeStruct((n_seq, 2 * shift, D_FF), F32),
        ],
        scratch_shapes=[pltpu.VMEM((hist_rows, D_FF), F32)],
        compiler_params=_cparams("arbitrary"),
        name="ffn",
    )(x, g, wg, wu, cw, cb, wd, hist, nf)


def _rot_tables(pos):
    half = RET_DK // 2
    inv = ROPE_BASE ** (-jnp.arange(half, dtype=F32) / half)
    ang = pos[:, None] * inv[None, :]
    c, s = jnp.cos(ang), jnp.sin(ang)
    return jnp.concatenate([c, c], axis=-1), jnp.concatenate([-s, s], axis=-1)


def _layer_weights(l, p):
    w_in = p["w_in"][l]
    return dict(
        norm1=p["norm1"][l][None, :],
        w_proj=w_in[:, :PROJ_COLS].astype(BF16),
        w_gate=w_in[:, PROJ_COLS:].astype(BF16),
        ret_gn=p["ret_gn"][l][None, :],
        sb_bias=p["sb_bias"][l],
        lru_cw=p["lru_conv_w"][l][:, None, :],
        lru_cb=p["lru_conv_b"][l][None, :],
        lru_wa=p["lru_w_a"][l].astype(BF16),
        lru_ba=p["lru_b_a"][l][None, :],
        lru_wx=p["lru_w_x"][l].astype(BF16),
        lru_bx=p["lru_b_x"][l][None, :],
        lru_lam=p["lru_lambda"][l][None, :],
        w_br_ret=p["w_br_ret"][l].astype(BF16),
        w_br_sb=p["w_br_sb"][l].astype(BF16),
        w_br_lru=p["w_br_lru"][l].astype(BF16),
        w_out=p["w_out"][l].astype(BF16),
        norm2=p["norm2"][l][None, :],
        w_ffn_gate=p["w_ffn_gate"][l].astype(BF16),
        w_ffn_up=p["w_ffn_up"][l].astype(BF16),
        ffn_cw=p["ffn_conv_w"][l][:, None, :],
        ffn_cb=p["ffn_conv_b"][l][None, :],
        w_ffn_down=p["w_ffn_down"][l].astype(BF16),
        norm_f=p["norm_f"][None, :],
    )


def _mix_and_ffn(x, o_ret, o_sb, o_lru, lw, ffn_hist, shift, tiles_per_seq, final_norm):
    x1 = _merge(x, o_ret, o_sb, o_lru, lw["norm1"], lw["w_gate"], lw["w_br_ret"], lw["w_br_sb"],
                lw["w_br_lru"], lw["w_out"])
    return _ffn(x1, lw["norm2"], lw["w_ffn_gate"], lw["w_ffn_up"], lw["ffn_cw"], lw["ffn_cb"],
                lw["w_ffn_down"], ffn_hist, lw["norm_f"], shift, tiles_per_seq, final_norm)


def _prompt_layer(x, lw, lg, cos, sin, batch, seq, final_norm):
    ret, sbq, sbk, sbv, lx = _proj(x, lw["norm1"], lw["w_proj"])
    o_ret, s_new = _ret_prompt(ret, lg, cos, sin, lw["ret_gn"], batch, seq)
    o_sb = _sb_prompt(sbq, sbk, sbv, lw["sb_bias"], batch, seq)
    lx3 = lx.reshape(batch, seq, LRU_WIDTH)
    h_tm, h_last = _lru(
        lx3.transpose(1, 0, 2),
        jnp.zeros((LRU_CONV - 1, batch, LRU_WIDTH), F32), jnp.zeros((batch, LRU_WIDTH), F32),
        lw["lru_cw"], lw["lru_cb"], lw["lru_wa"], lw["lru_ba"], lw["lru_wx"], lw["lru_bx"], lw["lru_lam"],
        tc=256)
    o_lru = h_tm.transpose(1, 0, 2).reshape(batch * seq, LRU_WIDTH).astype(BF16)
    x2, ffn_state = _mix_and_ffn(x, o_ret, o_sb, o_lru, lw, jnp.zeros((V7X_SUBLANES, D_FF), F32),
                                 shift=1, tiles_per_seq=seq // TOKEN_TILE, final_norm=final_norm)
    states = (
        sbk.reshape(batch, seq, SB_HEADS, SB_DH), sbv.reshape(batch, seq, SB_HEADS, SB_DH),
        s_new, h_last, lx3[:, seq - (LRU_CONV - 1):, :], ffn_state,
    )
    return x2, states


def _sample_layer(x, lw, lg, cos, sin, batch, steps, layer, page_table, cache_k, cache_v,
                  state_ret, state_lru_h, state_lru_conv, state_ffn_conv, final_norm):
    to_tok = lambda a: a.reshape(steps, batch, -1).transpose(1, 0, 2).reshape(batch * steps, -1)
    to_tm = lambda a: a.reshape(batch, steps, -1).transpose(1, 0, 2).reshape(steps * batch, -1)
    heads = lambda a: a.reshape(steps, batch, SB_HEADS, SB_DH).transpose(1, 0, 2, 3)
    ret, sbq, sbk, sbv, lx = _proj(x, lw["norm1"], lw["w_proj"])
    o_ret_tok, s_new = _ret_sample(to_tok(ret), lg, cos, sin, lw["ret_gn"], state_ret, steps)
    o_ret = to_tm(o_ret_tok)
    nq = steps * SB_HEADS
    k_new, v_new = heads(sbk), heads(sbv)
    o16 = _sb_sample(page_table, lw["sb_bias"], heads(sbq).reshape(batch, nq, SB_DH),
                     k_new.reshape(batch, nq, SB_DH), v_new.reshape(batch, nq, SB_DH),
                     cache_k, cache_v, layer, steps)
    o_sb = to_tm(o16.reshape(batch * steps, SB_W))
    lx3 = lx.reshape(steps, batch, LRU_WIDTH)
    h_tm, h_last = _lru(
        lx3, state_lru_conv.transpose(1, 0, 2), state_lru_h,
        lw["lru_cw"], lw["lru_cb"], lw["lru_wa"], lw["lru_ba"], lw["lru_wx"], lw["lru_bx"], lw["lru_lam"],
        tc=steps)
    o_lru = h_tm.reshape(steps * batch, LRU_WIDTH).astype(BF16)
    ffn_hist = state_ffn_conv.transpose(1, 0, 2).reshape((FFN_CONV - 1) * batch, D_FF)
    x2, ffn_state = _mix_and_ffn(x, o_ret, o_sb, o_lru, lw, ffn_hist, shift=batch, tiles_per_seq=1,
                                 final_norm=final_norm)
    states = (
        k_new, v_new, s_new, h_last,
        lx3[steps - (LRU_CONV - 1):].transpose(1, 0, 2),
        ffn_state.reshape(FFN_CONV - 1, batch, D_FF).transpose(1, 0, 2),
    )
    return x2, states


def kernel(x_prompt, x_sample, cache_sb_k, cache_sb_v, state_ret, state_lru_h, state_lru_conv, state_ffn_conv,
           page_table, norm1, w_in, ret_gn, sb_bias, lru_conv_w, lru_conv_b, lru_w_a, lru_b_a, lru_w_x, lru_b_x,
           lru_lambda, w_br_ret, w_br_sb, w_br_lru, w_out, norm2, w_ffn_gate, w_ffn_up, ffn_conv_w, ffn_conv_b,
           w_ffn_down, norm_f):
    params = dict(norm1=norm1, w_in=w_in, ret_gn=ret_gn, sb_bias=sb_bias, lru_conv_w=lru_conv_w,
                  lru_conv_b=lru_conv_b, lru_w_a=lru_w_a, lru_b_a=lru_b_a, lru_w_x=lru_w_x, lru_b_x=lru_b_x,
                  lru_lambda=lru_lambda, w_br_ret=w_br_ret, w_br_sb=w_br_sb, w_br_lru=w_br_lru, w_out=w_out,
                  norm2=norm2, w_ffn_gate=w_ffn_gate, w_ffn_up=w_ffn_up, ffn_conv_w=ffn_conv_w,
                  ffn_conv_b=ffn_conv_b, w_ffn_down=w_ffn_down, norm_f=norm_f)
    depth = w_in.shape[0]
    batch, seq, _ = x_prompt.shape
    dec_batch, steps, _ = x_sample.shape
    n_phys, page_size = cache_sb_k.shape[1], cache_sb_k.shape[2]
    past_len = page_table.shape[1] * page_size
    assert seq % TOKEN_TILE == 0 and dec_batch * steps == TOKEN_TILE and steps & (steps - 1) == 0
    assert steps >= LRU_CONV - 1 and steps >= FFN_CONV - 1

    lg = jnp.log(1.0 - 2.0 ** (-5.0 - jnp.arange(RET_HEADS, dtype=F32)))
    cos_p, sin_p = _rot_tables(jnp.arange(seq, dtype=F32))
    cos_s, sin_s = _rot_tables(past_len + jnp.arange(steps, dtype=F32))
    nb = 8
    cos_s, sin_s = jnp.tile(cos_s, (nb, 1)), jnp.tile(sin_s, (nb, 1))
    cache_k = cache_sb_k.reshape(depth, n_phys, page_size * SB_HEADS, SB_DH)
    cache_v = cache_sb_v.reshape(depth, n_phys, page_size * SB_HEADS, SB_DH)

    xp = x_prompt.reshape(batch * seq, D_MODEL)
    xs = x_sample.transpose(1, 0, 2).reshape(steps * dec_batch, D_MODEL)
    st_p, st_s = [], []
    for l in range(depth):
        lw = _layer_weights(l, params)
        last = l == depth - 1
        xp, st = _prompt_layer(xp, lw, lg, cos_p, sin_p, batch, seq, last)
        st_p.append(st)
        xs, st = _sample_layer(xs, lw, lg, cos_s, sin_s, dec_batch, steps, l, page_table, cache_k, cache_v,
                               state_ret[l], state_lru_h[l], state_lru_conv[l], state_ffn_conv[l], last)
        st_s.append(st)
    y_prompt = xp.reshape(batch, seq, D_MODEL)
    y_sample = xs.reshape(steps, dec_batch, D_MODEL).transpose(1, 0, 2)
    stack = lambda sts, j: jnp.stack([s[j] for s in sts], axis=0)
    return (y_prompt, y_sample,
            stack(st_p, 0), stack(st_p, 1), stack(st_s, 0), stack(st_s, 1),
            stack(st_p, 2), stack(st_s, 2),
            stack(st_p, 3), stack(st_s, 3),
            stack(st_p, 4), stack(st_s, 4),
            stack(st_p, 5), stack(st_s, 5))
```

```python
import functools

import jax
import jax.numpy as jnp
from jax import lax
from jax.experimental import pallas as pl
from jax.experimental.pallas import tpu as pltpu

F32 = jnp.float32
BF16 = jnp.bfloat16

D_MODEL = 1024
RET_HEADS = 4
RET_DK = 128
RET_DV = 128
RET_CHUNK = 128
ROPE_BASE = 10000.0
SB_HEADS = 4
SB_DH = 128
SB_TILE = 256
LRU_WIDTH = 512
LRU_BLOCKS = 4
LRU_BW = LRU_WIDTH // LRU_BLOCKS
LRU_CONV = 4
LRU_C = 8.0
N_BRANCH = 3
D_FF = 2816
FFN_CONV = 3
EPS = 1e-6

RET_W = RET_HEADS * RET_DK
SB_W = SB_HEADS * SB_DH
PROJ_COLS = 4 * RET_W + 3 * SB_W + LRU_WIDTH
GATE_COLS = N_BRANCH * D_MODEL

V7X_LANES = 128
V7X_SUBLANES = 8
V7X_VMEM_LIMIT_BYTES = 60000 * 1024

TOKEN_TILE = 512
FFN_COL_CHUNK = 256
LRU_ROW_CHUNK = 256


def _cparams(*sem):
    return pltpu.CompilerParams(dimension_semantics=sem, vmem_limit_bytes=V7X_VMEM_LIMIT_BYTES)


def _resident(shape):
    zeros = (0,) * len(shape)
    return pl.BlockSpec(shape, lambda *_: zeros, pipeline_mode=pl.Buffered(1))


def _mm(a, b):
    return jnp.dot(a, b, preferred_element_type=F32)


def _mm_nt(a, b):
    return lax.dot_general(a, b, (((1,), (1,)), ((), ())), preferred_element_type=F32)


def _mm_tn(a, b):
    return lax.dot_general(a, b, (((0,), (0,)), ((), ())), preferred_element_type=F32)


def _rms(x, g):
    return x * lax.rsqrt(jnp.mean(x * x, axis=-1, keepdims=True) + EPS) * g


LOG2E = 1.4426950408889634


def _neg_abs(x):
    bits = lax.bitcast_convert_type(x, jnp.uint32) | jnp.uint32(0x80000000)
    return lax.bitcast_convert_type(bits, F32)


def _softplus(z):
    return jnp.maximum(z, 0.0) + jnp.log(1.0 + jnp.exp(-jnp.abs(z)))


def _suffix_sum(lk, tri):
    hi = lk.astype(BF16)
    lo = (lk - hi.astype(F32)).astype(BF16)
    if tri.shape[0] == 2 * lk.shape[1]:
        return _mm(jnp.concatenate([hi, lo], axis=1), tri)
    return _mm(hi, tri) + _mm(lo, tri)


def _strict_lower(n):
    r = lax.broadcasted_iota(jnp.int32, (n, n), 0)
    c = lax.broadcasted_iota(jnp.int32, (n, n), 1)
    return (r > c).astype(BF16)


def _proj_kernel(x_ref, g_ref, w_ref, ret_ref, sbq_ref, sbk_ref, sbv_ref, lx_ref):
    xn = _rms(x_ref[...], g_ref[...]).astype(BF16)
    for c in range(4):
        ret_ref[:, c * RET_W:(c + 1) * RET_W] = _mm(xn, w_ref[:, c * RET_W:(c + 1) * RET_W])
    o = 4 * RET_W
    sbq_ref[...] = _mm(xn, w_ref[:, o:o + SB_W]).astype(BF16)
    sbk_ref[...] = _mm(xn, w_ref[:, o + SB_W:o + 2 * SB_W])
    sbv_ref[...] = _mm(xn, w_ref[:, o + 2 * SB_W:o + 3 * SB_W])
    lx_ref[...] = _mm(xn, w_ref[:, o + 3 * SB_W:o + 3 * SB_W + LRU_WIDTH])


def _proj(x, g, w):
    n = x.shape[0]
    tm = TOKEN_TILE
    row = lambda width: pl.BlockSpec((tm, width), lambda i: (i, 0))
    return pl.pallas_call(
        _proj_kernel,
        grid=(n // tm,),
        in_specs=[row(D_MODEL), _resident((1, D_MODEL)), _resident((D_MODEL, PROJ_COLS))],
        out_specs=[row(4 * RET_W), row(SB_W), row(SB_W), row(SB_W), row(LRU_WIDTH)],
        out_shape=[
            jax.ShapeDtypeStruct((n, 4 * RET_W), F32),
            jax.ShapeDtypeStruct((n, SB_W), BF16),
            jax.ShapeDtypeStruct((n, SB_W), F32),
            jax.ShapeDtypeStruct((n, SB_W), F32),
            jax.ShapeDtypeStruct((n, LRU_WIDTH), F32),
        ],
        compiler_params=_cparams("parallel"),
        name="proj",
    )(x, g, w)


def _rotary(x, cs, sn):
    return x * cs + pltpu.roll(x, RET_DK // 2, 1) * sn


def _group_norm_gate(o, gn, gate):
    mu = jnp.mean(o, axis=-1, keepdims=True)
    d = o - mu
    var = jnp.mean(d * d, axis=-1, keepdims=True)
    return d * lax.rsqrt(var + EPS) * gn * (gate * jax.nn.sigmoid(gate))


def _ret_prompt_kernel(lg_ref, q_ref, k_ref, v_ref, g_ref, cos_ref, sin_ref, gn_ref, o_ref, s_ref):
    lg = lg_ref[pl.program_id(1)]
    L = RET_CHUNK
    n_chunks = q_ref.shape[0] // L
    row = lax.broadcasted_iota(jnp.int32, (L, L), 0).astype(F32)
    col = lax.broadcasted_iota(jnp.int32, (L, L), 1).astype(F32)
    diff = row - col
    dmat = jnp.where(diff >= 0, jnp.exp(jnp.maximum(diff, 0.0) * lg), 0.0)
    q_dec = jnp.exp((row + 1.0) * lg)
    k_dec = jnp.exp((L - 1.0 - row) * lg)
    s_dec = jnp.exp(jnp.full((RET_DK, RET_DV), float(L), F32) * lg)
    gn = gn_ref[...]
    scale = RET_DK ** -0.5

    s = jnp.zeros((RET_DK, RET_DV), F32)
    for c in range(n_chunks):
        rows = slice(c * L, (c + 1) * L)
        cs, sn = cos_ref[rows, :], sin_ref[rows, :]
        q = _rotary(q_ref[rows, :], cs, sn)
        k = _rotary(k_ref[rows, :], cs, sn) * scale
        qb, kb, vb = q.astype(BF16), k.astype(BF16), v_ref[rows, :].astype(BF16)
        scores = _mm_nt(qb, kb) * dmat
        o = _mm(scores.astype(BF16), vb) + _mm(qb, s.astype(BF16)) * q_dec
        o_ref[rows, :] = _group_norm_gate(o, gn, g_ref[rows, :]).astype(BF16)
        s = s_dec * s + _mm_tn((k * k_dec).astype(BF16), vb)
    s_ref[0, 0] = s


def _ret_prompt(ret, lg, cos, sin, gn, batch, seq):
    H = RET_HEADS
    col = lambda j: pl.BlockSpec((seq, RET_DK), lambda b, h, j=j: (b, j * H + h))
    return pl.pallas_call(
        _ret_prompt_kernel,
        grid=(batch, H),
        in_specs=[
            pl.BlockSpec(memory_space=pltpu.SMEM),
            col(0), col(1), col(2), col(3),
            _resident((seq, RET_DK)), _resident((seq, RET_DK)),
            pl.BlockSpec((1, RET_DV), lambda b, h: (0, h)),
        ],
        out_specs=[
            pl.BlockSpec((seq, RET_DV), lambda b, h: (b, h)),
            pl.BlockSpec((1, 1, RET_DK, RET_DV), lambda b, h: (b, h, 0, 0)),
        ],
        out_shape=[
            jax.ShapeDtypeStruct((batch * seq, RET_W), BF16),
            jax.ShapeDtypeStruct((batch, H, RET_DK, RET_DV), F32),
        ],
        compiler_params=_cparams("parallel", "parallel"),
        name="ret_prompt",
    )(lg, ret, ret, ret, ret, cos, sin, gn)


def _ret_sample_kernel(lg_ref, q_ref, k_ref, v_ref, g_ref, cos_ref, sin_ref, gn_ref, st_ref, o_ref, so_ref,
                       *, nb, steps):
    R = nb * steps
    P = RET_DK
    shift = steps.bit_length() - 1
    scale = RET_DK ** -0.5
    ri = lax.broadcasted_iota(jnp.int32, (R, P), 0)
    ci = lax.broadcasted_iota(jnp.int32, (R, P), 1)
    rb, rt = ri >> shift, ri & (steps - 1)
    cb, ct = ci >> shift, ci & (steps - 1)
    same = (rb == cb) & (ci < R) & (ct <= rt)
    dt = jnp.maximum(rt - ct, 0).astype(F32)
    t_row = rt.astype(F32)
    wide_r = lax.broadcasted_iota(jnp.int32, (R, nb * RET_DV), 0) >> shift
    wide_c = lax.broadcasted_iota(jnp.int32, (R, nb * RET_DV), 1) >> (RET_DV.bit_length() - 1)
    own = wide_r == wide_c
    zpad = jnp.zeros((P - R, RET_DK), BF16)
    zpad_w = jnp.zeros((P - R, nb * RET_DV), BF16)
    cs, sn = cos_ref[...], sin_ref[...]
    for h in range(RET_HEADS):
        lg = lg_ref[h]
        sl = slice(h * RET_DK, (h + 1) * RET_DK)
        dmat = jnp.where(same, jnp.exp(dt * lg), 0.0)
        q_dec = jnp.exp((t_row + 1.0) * lg)
        k_dec = jnp.exp((steps - 1.0 - t_row) * lg)
        s_dec = jnp.exp(jnp.full((RET_DK, RET_DV), float(steps), F32) * lg)
        q = _rotary(q_ref[:, sl], cs, sn)
        k = _rotary(k_ref[:, sl], cs, sn) * scale
        v = v_ref[:, sl]
        qb = q.astype(BF16)
        k_pad = jnp.concatenate([k.astype(BF16), zpad], axis=0)
        v_pad = jnp.concatenate([v.astype(BF16), zpad], axis=0)
        scores = _mm_nt(qb, k_pad) * dmat
        intra = _mm(scores.astype(BF16), v_pad)
        s_cat = jnp.concatenate([st_ref[b, h] for b in range(nb)], axis=1)
        wide = jnp.where(own, _mm(qb, s_cat.astype(BF16)), 0.0)
        cross = wide[:, 0:RET_DV]
        for b in range(1, nb):
            cross = cross + wide[:, b * RET_DV:(b + 1) * RET_DV]
        o = intra + cross * q_dec
        o_ref[:, sl] = _group_norm_gate(o, gn_ref[:, sl], g_ref[:, sl]).astype(BF16)
        kd_pad = jnp.concatenate([(k * k_dec).astype(BF16), zpad], axis=0)
        v_wide = jnp.where(own, jnp.concatenate([v] * nb, axis=1), 0.0).astype(BF16)
        upd = _mm_tn(kd_pad, jnp.concatenate([v_wide, zpad_w], axis=0))
        for b in range(nb):
            so_ref[b, h] = s_dec * st_ref[b, h] + upd[:, b * RET_DV:(b + 1) * RET_DV]


def _ret_sample(ret_tok, lg, cos, sin, gn, state, steps):
    n = ret_tok.shape[0]
    batch = n // steps
    nb = 8
    R = nb * steps
    col = lambda j: pl.BlockSpec((R, RET_W), lambda i, j=j: (i, j))
    st_spec = pl.BlockSpec((nb, RET_HEADS, RET_DK, RET_DV), lambda i: (i, 0, 0, 0))
    return pl.pallas_call(
        functools.partial(_ret_sample_kernel, nb=nb, steps=steps),
        grid=(batch // nb,),
        in_specs=[
            pl.BlockSpec(memory_space=pltpu.SMEM),
            col(0), col(1), col(2), col(3),
            _resident((R, RET_DK)), _resident((R, RET_DK)), _resident((1, RET_W)),
            st_spec,
        ],
        out_specs=[pl.BlockSpec((R, RET_W), lambda i: (i, 0)), st_spec],
        out_shape=[
            jax.ShapeDtypeStruct((n, RET_W), BF16),
            jax.ShapeDtypeStruct(state.shape, F32),
        ],
        compiler_params=_cparams("parallel"),
        name="ret_sample",
    )(lg, ret_tok, ret_tok, ret_tok, ret_tok, cos, sin, gn, state)


def _sb_prompt_kernel(bias_ref, q_ref, k_ref, v_ref, o_ref, kb_ref, vb_ref):
    bias = bias_ref[pl.program_id(1)]
    n = SB_TILE
    n_blocks = q_ref.shape[0] // n
    scale = SB_DH ** -0.5
    kb_ref[...] = k_ref[...].astype(BF16)
    vb_ref[...] = v_ref[...].astype(BF16)
    tri = _strict_lower(n)
    tri2 = jnp.concatenate([tri, tri], axis=0)
    row = lax.broadcasted_iota(jnp.int32, (n, n), 0)
    col = lax.broadcasted_iota(jnp.int32, (n, n), 1)
    causal = col < row

    tiles = [(qi, c) for qi in range(n_blocks) for c in range(qi, -1, -1)]

    scale2, bias2 = scale * LOG2E, bias * LOG2E

    def logits(qi, c):
        return _mm_nt(q_ref[qi * n:(qi + 1) * n, :], kb_ref[c * n:(c + 1) * n, :]) * scale2 + bias2

    def keep(qi, c, z):
        sp = jnp.maximum(z, 0.0) + jnp.log(1.0 + jnp.exp2(_neg_abs(z))) * LOG2E
        drop = jnp.where(causal, sp, 0.0) if c == qi else sp
        return z - sp - _suffix_sum(drop, tri2), jnp.sum(drop, axis=1, keepdims=True)

    acc = run = None

    def weigh(qi, c, e, tot):
        nonlocal acc, run
        a = jnp.exp2(e if c == qi else e - run)
        if c == qi:
            a = jnp.where(causal, a, 0.0)
        term = _mm(a.astype(BF16), vb_ref[c * n:(c + 1) * n, :])
        acc = term if c == qi else acc + term
        run = tot if c == qi else run + tot
        if c == 0:
            o_ref[qi * n:(qi + 1) * n, :] = acc.astype(BF16)

    zs, es = {}, {}
    for i in range(len(tiles) + 2):
        if i < len(tiles):
            zs[i] = logits(*tiles[i])
        if 1 <= i <= len(tiles):
            es[i - 1] = keep(*tiles[i - 1], zs.pop(i - 1))
        if i >= 2:
            weigh(*tiles[i - 2], *es.pop(i - 2))


def _sb_prompt(sbq, sbk, sbv, bias, batch, seq):
    spec = pl.BlockSpec((seq, SB_DH), lambda b, h: (b, h))
    return pl.pallas_call(
        _sb_prompt_kernel,
        grid=(batch, SB_HEADS),
        in_specs=[pl.BlockSpec(memory_space=pltpu.SMEM), spec, spec, spec],
        out_specs=spec,
        out_shape=jax.ShapeDtypeStruct((batch * seq, SB_W), BF16),
        scratch_shapes=[pltpu.VMEM((seq, SB_DH), BF16), pltpu.VMEM((seq, SB_DH), BF16)],
        compiler_params=_cparams("parallel", "parallel"),
        name="sb_prompt",
    )(bias, sbq, sbk, sbv)


def _sb_sample_kernel(pt_ref, bias_ref, q_ref, kn_ref, vn_ref, k_hbm, v_hbm, o_ref, kbuf, vbuf, newk, newv, sem,
                      *, layer, n_pages, steps):
    b = pl.program_id(0)
    slot = b % 2
    H = SB_HEADS
    hbits = H.bit_length() - 1
    nq = steps * H
    page_rows = k_hbm.shape[2]
    n = SB_TILE
    n_chunks = n_pages * page_rows // (H * n)
    scale = SB_DH ** -0.5

    def copies(bb, sl):
        out = []
        for p in range(n_pages):
            pg = pt_ref[bb * n_pages + p]
            rows = pl.ds(p * page_rows, page_rows)
            out.append(pltpu.make_async_copy(k_hbm.at[layer, pg], kbuf.at[sl, rows, :], sem.at[0, sl]))
            out.append(pltpu.make_async_copy(v_hbm.at[layer, pg], vbuf.at[sl, rows, :], sem.at[1, sl]))
        return out

    def tile(buf, g):
        heads = [buf[slot, pl.ds(g * n * H + h, n, stride=H), :] for h in range(H)]
        return jnp.concatenate(heads, axis=1).astype(BF16)

    @pl.when(b == 0)
    def _():
        for c in copies(0, 0):
            c.start()

    @pl.when(b + 1 < pl.num_programs(0))
    def _():
        for c in copies(b + 1, 1 - slot):
            c.start()

    rw = lax.broadcasted_iota(jnp.int32, (nq, SB_W), 0)
    cw = lax.broadcasted_iota(jnp.int32, (nq, SB_W), 1)
    own = (rw & (H - 1)) == (cw >> (SB_DH.bit_length() - 1))
    q = q_ref[0]
    wq = jnp.where(own, jnp.concatenate([q] * H, axis=1), jnp.zeros((nq, SB_W), BF16))
    rn = lax.broadcasted_iota(jnp.int32, (nq, n), 0)
    cn = lax.broadcasted_iota(jnp.int32, (nq, n), 1)
    rh = rn & (H - 1)
    bias = jnp.where(rh == 0, bias_ref[0], jnp.where(rh == 1, bias_ref[1],
                     jnp.where(rh == 2, bias_ref[2], bias_ref[3])))
    tri = _strict_lower(n)

    @pl.when(b == 0)
    def _():
        newk[...] = jnp.zeros(newk.shape, F32)
        newv[...] = jnp.zeros(newv.shape, F32)

    newk[0:steps, :] = kn_ref[0]
    newv[0:steps, :] = vn_ref[0]
    valid = cn < (rn >> hbits)

    def scores(k_rows):
        z = _mm_nt(wq, k_rows) * scale + bias
        return z, _softplus(z)

    z, sp = scores(newk[...].astype(BF16))
    parts = [(z, sp, jnp.where(valid, -sp, 0.0), None)]

    for c in copies(b, slot):
        c.wait()

    for g in range(n_chunks - 1, -1, -1):
        z, sp = scores(tile(kbuf, g))
        parts.append((z, sp, -sp, g))

    split = []
    for _, _, lk, _ in parts:
        hi = lk.astype(BF16)
        split += [hi, (lk - hi.astype(F32)).astype(BF16)]
    suffix = _mm(jnp.concatenate(split, axis=0), tri)

    res = run = None
    for i, (z, sp, lk, g) in enumerate(parts):
        e = z - sp + suffix[2 * i * nq:(2 * i + 1) * nq] + suffix[(2 * i + 1) * nq:(2 * i + 2) * nq]
        if run is not None:
            e = e + run
        a = jnp.exp(e)
        if g is None:
            a = jnp.where(valid, a, 0.0)
            v_rows = newv[...].astype(BF16)
        else:
            v_rows = tile(vbuf, g)
        term = _mm(a.astype(BF16), v_rows)
        tot = jnp.sum(lk, axis=1, keepdims=True)
        res = term if res is None else res + term
        run = tot if run is None else run + tot

    out = jnp.where(own, res, 0.0)
    acc = out[:, 0:SB_DH]
    for h in range(1, H):
        acc = acc + out[:, h * SB_DH:(h + 1) * SB_DH]
    o_ref[0] = acc.astype(BF16)


def _sb_sample(page_table, bias, q16, k_new, v_new, cache_k, cache_v, layer, steps):
    batch, n_pages = page_table.shape
    nq = steps * SB_HEADS
    depth, n_phys, page_size = cache_k.shape[:3]
    cache_k = cache_k.reshape(depth, n_phys, page_size * SB_HEADS, SB_DH)
    cache_v = cache_v.reshape(depth, n_phys, page_size * SB_HEADS, SB_DH)
    past_rows = n_pages * page_size * SB_HEADS
    row = pl.BlockSpec((1, nq, SB_DH), lambda b, pt: (b, 0, 0))
    new = pl.BlockSpec((1, steps, SB_W), lambda b, pt: (b, 0, 0))
    return pl.pallas_call(
        functools.partial(_sb_sample_kernel, layer=layer, n_pages=n_pages, steps=steps),
        grid_spec=pltpu.PrefetchScalarGridSpec(
            num_scalar_prefetch=1,
            grid=(batch,),
            in_specs=[
                pl.BlockSpec(memory_space=pltpu.SMEM),
                row, new, new,
                pl.BlockSpec(memory_space=pl.ANY),
                pl.BlockSpec(memory_space=pl.ANY),
            ],
            out_specs=row,
            scratch_shapes=[
                pltpu.VMEM((2, past_rows, SB_DH), F32),
                pltpu.VMEM((2, past_rows, SB_DH), F32),
                pltpu.VMEM((SB_TILE, SB_W), F32),
                pltpu.VMEM((SB_TILE, SB_W), F32),
                pltpu.SemaphoreType.DMA((2, 2)),
            ],
        ),
        out_shape=jax.ShapeDtypeStruct((batch, nq, SB_DH), BF16),
        compiler_params=_cparams("arbitrary"),
        name="sb_sample",
    )(page_table.reshape(-1), bias, q16, k_new, v_new, cache_k, cache_v)


def _lru_kernel(x_ref, hist_ref, h0_ref, cw_ref, cb_ref, wa_ref, ba_ref, wx_ref, bx_ref, lam_ref,
                h_ref, hl_ref, tail, hc, a_s, b_s, *, sub):
    tc, rows, width = x_ref.shape
    taps = LRU_CONV

    @pl.when(pl.program_id(0) == 0)
    def _():
        tail[...] = hist_ref[...]
        hc[...] = h0_ref[...]

    lam = lam_ref[...]
    log_lam = -(jnp.maximum(-lam, 0.0) + jnp.log1p(jnp.exp(-jnp.abs(lam))))
    for j in range(tc // sub):
        t0 = j * sub
        if t0 < taps - 1:
            xe = jnp.concatenate([tail[t0:taps - 1], x_ref[0:t0 + sub]], axis=0)
        else:
            xe = x_ref[t0 - (taps - 1):t0 + sub]
        xc = cb_ref[...] + xe[0:sub] * cw_ref[0]
        for i in range(1, taps):
            xc = xc + xe[i:i + sub] * cw_ref[i]
        xc = xc.reshape(sub * rows, width)
        r_parts, i_parts = [], []
        for n in range(LRU_BLOCKS):
            blk = xc[:, n * LRU_BW:(n + 1) * LRU_BW].astype(BF16)
            r_parts.append(_mm(blk, wa_ref[n]))
            i_parts.append(_mm(blk, wx_ref[n]))
        r = jax.nn.sigmoid(jnp.concatenate(r_parts, axis=1) + ba_ref[...])
        gate_i = jax.nn.sigmoid(jnp.concatenate(i_parts, axis=1) + bx_ref[...])
        log_a = LRU_C * r * log_lam
        a_s[t0:t0 + sub] = jnp.exp(log_a).reshape(sub, rows, width)
        one_minus_a2 = -jnp.tanh(log_a) * (jnp.exp(2.0 * log_a) + 1.0)
        b_s[t0:t0 + sub] = (jnp.sqrt(one_minus_a2) * (gate_i * xc)).reshape(sub, rows, width)
    tail[...] = x_ref[tc - (taps - 1):tc]

    def step(t, h):
        h = a_s[t] * h + b_s[t]
        h_ref[t] = h
        return h

    h = lax.fori_loop(0, tc, step, hc[...], unroll=min(tc, 8))
    hc[...] = h
    hl_ref[...] = h


def _lru(x_tm, hist, h0, cw, cb, wa, ba, wx, bx, lam, tc):
    steps, rows, width = x_tm.shape
    sub = max(1, LRU_ROW_CHUNK // rows)
    blk = pl.BlockSpec((tc, rows, width), lambda i: (i, 0, 0))
    vec = _resident((1, width))
    return pl.pallas_call(
        functools.partial(_lru_kernel, sub=sub),
        grid=(steps // tc,),
        in_specs=[
            blk, _resident((LRU_CONV - 1, rows, width)), _resident((rows, width)),
            _resident((LRU_CONV, 1, width)), vec,
            _resident((LRU_BLOCKS, LRU_BW, LRU_BW)), vec,
            _resident((LRU_BLOCKS, LRU_BW, LRU_BW)), vec, vec,
        ],
        out_specs=[blk, pl.BlockSpec((rows, width), lambda i: (0, 0))],
        out_shape=[
            jax.ShapeDtypeStruct((steps, rows, width), F32),
            jax.ShapeDtypeStruct((rows, width), F32),
        ],
        scratch_shapes=[
            pltpu.VMEM((LRU_CONV - 1, rows, width), F32),
            pltpu.VMEM((rows, width), F32),
            pltpu.VMEM((tc, rows, width), F32),
            pltpu.VMEM((tc, rows, width), F32),
        ],
        compiler_params=_cparams("arbitrary"),
        name="lru",
    )(x_tm, hist, h0, cw, cb, wa, ba, wx, bx, lam)


def _merge_kernel(x_ref, oret_ref, osb_ref, olru_ref, g_ref, wgl_ref, wr_ref, ws_ref, wl_ref, wo_ref,
                  out_ref, m_ref):
    x = x_ref[...]
    xn = _rms(x, g_ref[...]).astype(BF16)
    branches = ((oret_ref, wr_ref), (osb_ref, ws_ref), (olru_ref, wl_ref))
    cw = 256
    for c in range(D_MODEL // cw):
        cols = slice(c * cw, (c + 1) * cw)
        m = None
        for j, (o_ref, w_ref) in enumerate(branches):
            gate = jax.nn.sigmoid(_mm(xn, wgl_ref[:, j * D_MODEL + c * cw:j * D_MODEL + (c + 1) * cw]))
            term = gate * _mm(o_ref[...], w_ref[:, cols])
            m = term if m is None else m + term
        m_ref[:, cols] = m.astype(BF16)
    out_ref[...] = x + _mm(m_ref[...], wo_ref[...])


def _merge(x, o_ret, o_sb, o_lru, g, wgl, wr, ws, wl, wo):
    n = x.shape[0]
    tm = TOKEN_TILE
    row = lambda width: pl.BlockSpec((tm, width), lambda i: (i, 0))
    return pl.pallas_call(
        _merge_kernel,
        grid=(n // tm,),
        in_specs=[
            row(D_MODEL), row(RET_W), row(SB_W), row(LRU_WIDTH),
            _resident((1, D_MODEL)), _resident((D_MODEL, GATE_COLS)),
            _resident((RET_W, D_MODEL)), _resident((SB_W, D_MODEL)), _resident((LRU_WIDTH, D_MODEL)),
            _resident((D_MODEL, D_MODEL)),
        ],
        out_specs=row(D_MODEL),
        out_shape=jax.ShapeDtypeStruct((n, D_MODEL), F32),
        scratch_shapes=[pltpu.VMEM((tm, D_MODEL), BF16)],
        compiler_params=_cparams("parallel"),
        name="merge",
    )(x, o_ret, o_sb, o_lru, g, wgl, wr, ws, wl, wo)


def _ffn_kernel(x_ref, g_ref, wg_ref, wu_ref, cw_ref, cb_ref, wd_ref, hist_ref, nf_ref, out_ref, st_ref,
                carry, *, shift, tiles_per_seq, final_norm):
    tm = x_ref.shape[0]
    hist_rows = carry.shape[0]
    i = pl.program_id(0)

    @pl.when(i % tiles_per_seq == 0)
    def _():
        carry[...] = hist_ref[...]

    x = x_ref[...]
    xn = _rms(x, g_ref[...]).astype(BF16)
    acc = jnp.zeros((tm, D_MODEL), F32)
    S = V7X_SUBLANES
    for c in range(D_FF // FFN_COL_CHUNK):
        cols = slice(c * FFN_COL_CHUNK, (c + 1) * FFN_COL_CHUNK)
        g = _mm(xn, wg_ref[:, cols])
        u = _mm(xn, wu_ref[:, cols])
        hist = carry[:, cols]
        if shift == 1:
            ext = jnp.concatenate([hist, g[0:S]], axis=0)
            p1 = jnp.concatenate([pltpu.roll(ext, 1, 0)[S:2 * S], pltpu.roll(g, 1, 0)[S:]], axis=0)
            p2 = jnp.concatenate([pltpu.roll(ext, 2, 0)[S:2 * S], pltpu.roll(g, 2, 0)[S:]], axis=0)
        else:
            p1 = jnp.concatenate([hist[shift:2 * shift], g[0:tm - shift]], axis=0)
            p2 = jnp.concatenate([hist[0:2 * shift], g[0:tm - 2 * shift]], axis=0)
        carry[:, cols] = g[tm - hist_rows:tm]
        gc = cb_ref[:, cols] + p2 * cw_ref[0, :, cols] + p1 * cw_ref[1, :, cols] + g * cw_ref[2, :, cols]
        hcol = (jax.nn.gelu(gc) * u).astype(BF16)
        acc = acc + _mm(hcol, wd_ref[cols, :])
    st_ref[...] = carry[hist_rows - 2 * shift:hist_rows, :].reshape(st_ref.shape)
    y = x + acc
    if final_norm:
        y = _rms(y, nf_ref[...])
    out_ref[...] = y


def _ffn(x, g, wg, wu, cw, cb, wd, hist, nf, shift, tiles_per_seq, final_norm):
    n = x.shape[0]
    tm = TOKEN_TILE
    n_seq = n // (tm * tiles_per_seq)
    hist_rows = hist.shape[0]
    row = pl.BlockSpec((tm, D_MODEL), lambda i: (i, 0))
    return pl.pallas_call(
        functools.partial(_ffn_kernel, shift=shift, tiles_per_seq=tiles_per_seq, final_norm=final_norm),
        grid=(n // tm,),
        in_specs=[
            row, _resident((1, D_MODEL)),
            _resident((D_MODEL, D_FF)), _resident((D_MODEL, D_FF)),
            _resident((FFN_CONV, 1, D_FF)), _resident((1, D_FF)),
            _resident((D_FF, D_MODEL)),
            _resident((hist_rows, D_FF)), _resident((1, D_MODEL)),
        ],
        out_specs=[row, pl.BlockSpec((1, 2 * shift, D_FF), lambda i: (i // tiles_per_seq, 0, 0))],
        out_shape=[
            jax.ShapeDtypeStruct((n, D_MODEL), F32),
            jax.ShapeDtypeStruct((n_seq, 2 * shift, D_FF), F32),
        ],
        scratch_shapes=[pltpu.VMEM((hist_rows, D_FF), F32)],
        compiler_params=_cparams("arbitrary"),
        name="ffn",
    )(x, g, wg, wu, cw, cb, wd, hist, nf)


def _rot_tables(pos):
    half = RET_DK // 2
    inv = ROPE_BASE ** (-jnp.arange(half, dtype=F32) / half)
    ang = pos[:, None] * inv[None, :]
    c, s = jnp.cos(ang), jnp.sin(ang)
    return jnp.concatenate([c, c], axis=-1), jnp.concatenate([-s, s], axis=-1)


def _layer_weights(l, p):
    w_in = p["w_in"][l]
    return dict(
        norm1=p["norm1"][l][None, :],
        w_proj=w_in[:, :PROJ_COLS].astype(BF16),
        w_gate=w_in[:, PROJ_COLS:].astype(BF16),
        ret_gn=p["ret_gn"][l][None, :],
        sb_bias=p["sb_bias"][l],
        lru_cw=p["lru_conv_w"][l][:, None, :],
        lru_cb=p["lru_conv_b"][l][None, :],
        lru_wa=p["lru_w_a"][l].astype(BF16),
        lru_ba=p["lru_b_a"][l][None, :],
        lru_wx=p["lru_w_x"][l].astype(BF16),
        lru_bx=p["lru_b_x"][l][None, :],
        lru_lam=p["lru_lambda"][l][None, :],
        w_br_ret=p["w_br_ret"][l].astype(BF16),
        w_br_sb=p["w_br_sb"][l].astype(BF16),
        w_br_lru=p["w_br_lru"][l].astype(BF16),
        w_out=p["w_out"][l].astype(BF16),
        norm2=p["norm2"][l][None, :],
        w_ffn_gate=p["w_ffn_gate"][l].astype(BF16),
        w_ffn_up=p["w_ffn_up"][l].astype(BF16),
        ffn_cw=p["ffn_conv_w"][l][:, None, :],
        ffn_cb=p["ffn_conv_b"][l][None, :],
        w_ffn_down=p["w_ffn_down"][l].astype(BF16),
        norm_f=p["norm_f"][None, :],
    )


def _mix_and_ffn(x, o_ret, o_sb, o_lru, lw, ffn_hist, shift, tiles_per_seq, final_norm):
    x1 = _merge(x, o_ret, o_sb, o_lru, lw["norm1"], lw["w_gate"], lw["w_br_ret"], lw["w_br_sb"],
                lw["w_br_lru"], lw["w_out"])
    return _ffn(x1, lw["norm2"], lw["w_ffn_gate"], lw["w_ffn_up"], lw["ffn_cw"], lw["ffn_cb"],
                lw["w_ffn_down"], ffn_hist, lw["norm_f"], shift, tiles_per_seq, final_norm)


def _prompt_layer(x, lw, lg, cos, sin, batch, seq, final_norm):
    ret, sbq, sbk, sbv, lx = _proj(x, lw["norm1"], lw["w_proj"])
    o_ret, s_new = _ret_prompt(ret, lg, cos, sin, lw["ret_gn"], batch, seq)
    o_sb = _sb_prompt(sbq, sbk, sbv, lw["sb_bias"], batch, seq)
    lx3 = lx.reshape(batch, seq, LRU_WIDTH)
    h_tm, h_last = _lru(
        lx3.transpose(1, 0, 2),
        jnp.zeros((LRU_CONV - 1, batch, LRU_WIDTH), F32), jnp.zeros((batch, LRU_WIDTH), F32),
        lw["lru_cw"], lw["lru_cb"], lw["lru_wa"], lw["lru_ba"], lw["lru_wx"], lw["lru_bx"], lw["lru_lam"],
        tc=256)
    o_lru = h_tm.transpose(1, 0, 2).reshape(batch * seq, LRU_WIDTH).astype(BF16)
    x2, ffn_state = _mix_and_ffn(x, o_ret, o_sb, o_lru, lw, jnp.zeros((V7X_SUBLANES, D_FF), F32),
                                 shift=1, tiles_per_seq=seq // TOKEN_TILE, final_norm=final_norm)
    states = (
        sbk.reshape(batch, seq, SB_HEADS, SB_DH), sbv.reshape(batch, seq, SB_HEADS, SB_DH),
        s_new, h_last, lx3[:, seq - (LRU_CONV - 1):, :], ffn_state,
    )
    return x2, states


def _sample_layer(x, lw, lg, cos, sin, batch, steps, layer, page_table, cache_k, cache_v,
                  state_ret, state_lru_h, state_lru_conv, state_ffn_conv, final_norm):
    to_tok = lambda a: a.reshape(steps, batch, -1).transpose(1, 0, 2).reshape(batch * steps, -1)
    to_tm = lambda a: a.reshape(batch, steps, -1).transpose(1, 0, 2).reshape(steps * batch, -1)
    heads = lambda a: a.reshape(steps, batch, SB_HEADS, SB_DH).transpose(1, 0, 2, 3)
    ret, sbq, sbk, sbv, lx = _proj(x, lw["norm1"], lw["w_proj"])
    o_ret_tok, s_new = _ret_sample(to_tok(ret), lg, cos, sin, lw["ret_gn"], state_ret, steps)
    o_ret = to_tm(o_ret_tok)
    nq = steps * SB_HEADS
    k_new, v_new = heads(sbk), heads(sbv)
    o16 = _sb_sample(page_table, lw["sb_bias"], heads(sbq).reshape(batch, nq, SB_DH),
                     k_new.reshape(batch, steps, SB_W), v_new.reshape(batch, steps, SB_W),
                     cache_k, cache_v, layer, steps)
    o_sb = to_tm(o16.reshape(batch * steps, SB_W))
    lx3 = lx.reshape(steps, batch, LRU_WIDTH)
    h_tm, h_last = _lru(
        lx3, state_lru_conv.transpose(1, 0, 2), state_lru_h,
        lw["lru_cw"], lw["lru_cb"], lw["lru_wa"], lw["lru_ba"], lw["lru_wx"], lw["lru_bx"], lw["lru_lam"],
        tc=steps)
    o_lru = h_tm.reshape(steps * batch, LRU_WIDTH).astype(BF16)
    ffn_hist = state_ffn_conv.transpose(1, 0, 2).reshape((FFN_CONV - 1) * batch, D_FF)
    x2, ffn_state = _mix_and_ffn(x, o_ret, o_sb, o_lru, lw, ffn_hist, shift=batch, tiles_per_seq=1,
                                 final_norm=final_norm)
    states = (
        k_new, v_new, s_new, h_last,
        lx3[steps - (LRU_CONV - 1):].transpose(1, 0, 2),
        ffn_state.reshape(FFN_CONV - 1, batch, D_FF).transpose(1, 0, 2),
    )
    return x2, states


def kernel(x_prompt, x_sample, cache_sb_k, cache_sb_v, state_ret, state_lru_h, state_lru_conv, state_ffn_conv,
           page_table, norm1, w_in, ret_gn, sb_bias, lru_conv_w, lru_conv_b, lru_w_a, lru_b_a, lru_w_x, lru_b_x,
           lru_lambda, w_br_ret, w_br_sb, w_br_lru, w_out, norm2, w_ffn_gate, w_ffn_up, ffn_conv_w, ffn_conv_b,
           w_ffn_down, norm_f):
    params = dict(norm1=norm1, w_in=w_in, ret_gn=ret_gn, sb_bias=sb_bias, lru_conv_w=lru_conv_w,
                  lru_conv_b=lru_conv_b, lru_w_a=lru_w_a, lru_b_a=lru_b_a, lru_w_x=lru_w_x, lru_b_x=lru_b_x,
                  lru_lambda=lru_lambda, w_br_ret=w_br_ret, w_br_sb=w_br_sb, w_br_lru=w_br_lru, w_out=w_out,
                  norm2=norm2, w_ffn_gate=w_ffn_gate, w_ffn_up=w_ffn_up, ffn_conv_w=ffn_conv_w,
                  ffn_conv_b=ffn_conv_b, w_ffn_down=w_ffn_down, norm_f=norm_f)
    depth = w_in.shape[0]
    batch, seq, _ = x_prompt.shape
    dec_batch, steps, _ = x_sample.shape
    past_len = page_table.shape[1] * cache_sb_k.shape[2]
    assert seq % TOKEN_TILE == 0 and dec_batch * steps == TOKEN_TILE and steps & (steps - 1) == 0
    assert steps >= LRU_CONV - 1 and steps >= FFN_CONV - 1

    lg = jnp.log(1.0 - 2.0 ** (-5.0 - jnp.arange(RET_HEADS, dtype=F32)))
    cos_p, sin_p = _rot_tables(jnp.arange(seq, dtype=F32))
    cos_s, sin_s = _rot_tables(past_len + jnp.arange(steps, dtype=F32))
    nb = 8
    cos_s, sin_s = jnp.tile(cos_s, (nb, 1)), jnp.tile(sin_s, (nb, 1))

    xp = x_prompt.reshape(batch * seq, D_MODEL)
    xs = x_sample.transpose(1, 0, 2).reshape(steps * dec_batch, D_MODEL)
    st_p, st_s = [], []
    for l in range(depth):
        lw = _layer_weights(l, params)
        last = l == depth - 1
        xp, st = _prompt_layer(xp, lw, lg, cos_p, sin_p, batch, seq, last)
        st_p.append(st)
        xs, st = _sample_layer(xs, lw, lg, cos_s, sin_s, dec_batch, steps, l, page_table, cache_sb_k, cache_sb_v,
                               state_ret[l], state_lru_h[l], state_lru_conv[l], state_ffn_conv[l], last)
        st_s.append(st)
    y_prompt = xp.reshape(batch, seq, D_MODEL)
    y_sample = xs.reshape(steps, dec_batch, D_MODEL).transpose(1, 0, 2)
    stack = lambda sts, j: jnp.stack([s[j] for s in sts], axis=0)
    return (y_prompt, y_sample,
            stack(st_p, 0), stack(st_p, 1), stack(st_s, 0), stack(st_s, 1),
            stack(st_p, 2), stack(st_s, 2),
            stack(st_p, 3), stack(st_s, 3),
            stack(st_p, 4), stack(st_s, 4),
            stack(st_p, 5), stack(st_s, 5))
```

```python
import functools

import jax
import jax.numpy as jnp
from jax import lax
from jax.experimental import pallas as pl
from jax.experimental.pallas import tpu as pltpu

F32 = jnp.float32
BF16 = jnp.bfloat16

D_MODEL = 1024
RET_HEADS = 4
RET_DK = 128
RET_DV = 128
RET_CHUNK = 128
ROPE_BASE = 10000.0
SB_HEADS = 4
SB_DH = 128
SB_TILE = 256
LRU_WIDTH = 512
LRU_BLOCKS = 4
LRU_BW = LRU_WIDTH // LRU_BLOCKS
LRU_CONV = 4
LRU_C = 8.0
N_BRANCH = 3
D_FF = 2816
FFN_CONV = 3
EPS = 1e-6

RET_W = RET_HEADS * RET_DK
SB_W = SB_HEADS * SB_DH
PROJ_COLS = 4 * RET_W + 3 * SB_W + LRU_WIDTH
GATE_COLS = N_BRANCH * D_MODEL

V7X_LANES = 128
V7X_SUBLANES = 8
V7X_VMEM_LIMIT_BYTES = 60000 * 1024

TOKEN_TILE = 512
FFN_COL_CHUNK = 256
LRU_ROW_CHUNK = 256


def _cparams(*sem):
    return pltpu.CompilerParams(dimension_semantics=sem, vmem_limit_bytes=V7X_VMEM_LIMIT_BYTES)


def _resident(shape):
    zeros = (0,) * len(shape)
    return pl.BlockSpec(shape, lambda *_: zeros, pipeline_mode=pl.Buffered(1))


def _mm(a, b):
    return jnp.dot(a, b, preferred_element_type=F32)


def _mm_nt(a, b):
    return lax.dot_general(a, b, (((1,), (1,)), ((), ())), preferred_element_type=F32)


def _mm_tn(a, b):
    return lax.dot_general(a, b, (((0,), (0,)), ((), ())), preferred_element_type=F32)


def _rms(x, g):
    return x * lax.rsqrt(jnp.mean(x * x, axis=-1, keepdims=True) + EPS) * g


LOG2E = 1.4426950408889634


def _neg_abs(x):
    bits = lax.bitcast_convert_type(x, jnp.uint32) | jnp.uint32(0x80000000)
    return lax.bitcast_convert_type(bits, F32)


def _softplus(z):
    return jnp.maximum(z, 0.0) + jnp.log(1.0 + jnp.exp(-jnp.abs(z)))


def _suffix_sum(lk, tri):
    hi = lk.astype(BF16)
    lo = (lk - hi.astype(F32)).astype(BF16)
    if tri.shape[0] == 2 * lk.shape[1]:
        return _mm(jnp.concatenate([hi, lo], axis=1), tri)
    return _mm(hi, tri) + _mm(lo, tri)


def _strict_lower(n):
    r = lax.broadcasted_iota(jnp.int32, (n, n), 0)
    c = lax.broadcasted_iota(jnp.int32, (n, n), 1)
    return (r > c).astype(BF16)


def _proj_kernel(x_ref, g_ref, w_ref, ret_ref, sbq_ref, sbk_ref, sbv_ref, lx_ref):
    xn = _rms(x_ref[...], g_ref[...]).astype(BF16)
    for c in range(4):
        ret_ref[:, c * RET_W:(c + 1) * RET_W] = _mm(xn, w_ref[:, c * RET_W:(c + 1) * RET_W])
    o = 4 * RET_W
    sbq_ref[...] = _mm(xn, w_ref[:, o:o + SB_W]).astype(BF16)
    lx_ref[...] = _mm(xn, w_ref[:, o + 3 * SB_W:o + 3 * SB_W + LRU_WIDTH])
    for ref, lo in ((sbk_ref, o + SB_W), (sbv_ref, o + 2 * SB_W)):
        kv = _mm(xn, w_ref[:, lo:lo + SB_W])
        if len(ref.shape) == 2:
            ref[...] = kv
        else:
            tm = kv.shape[0]
            for h in range(SB_HEADS):
                ref[0, pl.ds(h, tm, stride=SB_HEADS), :] = kv[:, h * SB_DH:(h + 1) * SB_DH]
            for later in range(1, ref.shape[0]):
                ref[later] = jnp.zeros(ref.shape[1:], F32)


def _proj_kernel_stacked(x_ref, g_ref, w_ref, prev_k, prev_v, *out_refs):
    _proj_kernel(x_ref, g_ref, w_ref, *out_refs)


def _proj(x, g, w, stacked=None):
    n = x.shape[0]
    tm = TOKEN_TILE
    row = lambda width: pl.BlockSpec((tm, width), lambda i: (i, 0))
    in_specs = [row(D_MODEL), _resident((1, D_MODEL)), _resident((D_MODEL, PROJ_COLS))]
    args = [x, g, w]
    body, aliases = _proj_kernel, {}
    if stacked is None:
        kv_spec, kv_shape = row(SB_W), jax.ShapeDtypeStruct((n, SB_W), F32)
    else:
        layer, depth, prev_k, prev_v = stacked
        kv_shape = jax.ShapeDtypeStruct((depth, n * SB_HEADS, SB_DH), F32)
        if prev_k is None:
            assert layer == 0
            kv_spec = pl.BlockSpec((depth, tm * SB_HEADS, SB_DH), lambda i: (0, i, 0))
        else:
            kv_spec = pl.BlockSpec((1, tm * SB_HEADS, SB_DH), lambda i: (layer, i, 0))
            body, aliases = _proj_kernel_stacked, {3: 2, 4: 3}
            in_specs += [pl.BlockSpec(memory_space=pl.ANY)] * 2
            args += [prev_k, prev_v]
    return pl.pallas_call(
        body,
        grid=(n // tm,),
        in_specs=in_specs,
        out_specs=[row(4 * RET_W), row(SB_W), kv_spec, kv_spec, row(LRU_WIDTH)],
        out_shape=[
            jax.ShapeDtypeStruct((n, 4 * RET_W), F32),
            jax.ShapeDtypeStruct((n, SB_W), BF16),
            kv_shape, kv_shape,
            jax.ShapeDtypeStruct((n, LRU_WIDTH), F32),
        ],
        input_output_aliases=aliases,
        compiler_params=_cparams("parallel"),
        name="proj",
    )(*args)


def _rotary(x, cs, sn):
    return x * cs + pltpu.roll(x, RET_DK // 2, 1) * sn


def _group_norm_gate(o, gn, gate):
    mu = jnp.mean(o, axis=-1, keepdims=True)
    d = o - mu
    var = jnp.mean(d * d, axis=-1, keepdims=True)
    return d * lax.rsqrt(var + EPS) * gn * (gate * jax.nn.sigmoid(gate))


def _ret_prompt_kernel(lg_ref, q_ref, k_ref, v_ref, g_ref, cos_ref, sin_ref, gn_ref, o_ref, s_ref):
    lg = lg_ref[pl.program_id(1)]
    L = RET_CHUNK
    n_chunks = q_ref.shape[0] // L
    row = lax.broadcasted_iota(jnp.int32, (L, L), 0).astype(F32)
    col = lax.broadcasted_iota(jnp.int32, (L, L), 1).astype(F32)
    diff = row - col
    dmat = jnp.where(diff >= 0, jnp.exp(jnp.maximum(diff, 0.0) * lg), 0.0)
    q_dec = jnp.exp((row + 1.0) * lg)
    k_dec = jnp.exp((L - 1.0 - row) * lg)
    s_dec = jnp.exp(jnp.full((RET_DK, RET_DV), float(L), F32) * lg)
    gn = gn_ref[...]
    scale = RET_DK ** -0.5

    s = jnp.zeros((RET_DK, RET_DV), F32)
    for c in range(n_chunks):
        rows = slice(c * L, (c + 1) * L)
        cs, sn = cos_ref[rows, :], sin_ref[rows, :]
        q = _rotary(q_ref[rows, :], cs, sn)
        k = _rotary(k_ref[rows, :], cs, sn) * scale
        qb, kb, vb = q.astype(BF16), k.astype(BF16), v_ref[rows, :].astype(BF16)
        scores = _mm_nt(qb, kb) * dmat
        o = _mm(scores.astype(BF16), vb) + _mm(qb, s.astype(BF16)) * q_dec
        o_ref[rows, :] = _group_norm_gate(o, gn, g_ref[rows, :]).astype(BF16)
        s = s_dec * s + _mm_tn((k * k_dec).astype(BF16), vb)
    s_ref[0, 0] = s


def _ret_prompt(ret, lg, cos, sin, gn, batch, seq):
    H = RET_HEADS
    col = lambda j: pl.BlockSpec((seq, RET_DK), lambda b, h, j=j: (b, j * H + h))
    return pl.pallas_call(
        _ret_prompt_kernel,
        grid=(batch, H),
        in_specs=[
            pl.BlockSpec(memory_space=pltpu.SMEM),
            col(0), col(1), col(2), col(3),
            _resident((seq, RET_DK)), _resident((seq, RET_DK)),
            pl.BlockSpec((1, RET_DV), lambda b, h: (0, h)),
        ],
        out_specs=[
            pl.BlockSpec((seq, RET_DV), lambda b, h: (b, h)),
            pl.BlockSpec((1, 1, RET_DK, RET_DV), lambda b, h: (b, h, 0, 0)),
        ],
        out_shape=[
            jax.ShapeDtypeStruct((batch * seq, RET_W), BF16),
            jax.ShapeDtypeStruct((batch, H, RET_DK, RET_DV), F32),
        ],
        compiler_params=_cparams("parallel", "parallel"),
        name="ret_prompt",
    )(lg, ret, ret, ret, ret, cos, sin, gn)


def _ret_sample_kernel(lg_ref, q_ref, k_ref, v_ref, g_ref, cos_ref, sin_ref, gn_ref, st_ref, o_ref, so_ref,
                       *, nb, steps):
    R = nb * steps
    P = RET_DK
    shift = steps.bit_length() - 1
    scale = RET_DK ** -0.5
    ri = lax.broadcasted_iota(jnp.int32, (R, P), 0)
    ci = lax.broadcasted_iota(jnp.int32, (R, P), 1)
    rb, rt = ri >> shift, ri & (steps - 1)
    cb, ct = ci >> shift, ci & (steps - 1)
    same = (rb == cb) & (ci < R) & (ct <= rt)
    dt = jnp.maximum(rt - ct, 0).astype(F32)
    t_row = rt.astype(F32)
    wide_r = lax.broadcasted_iota(jnp.int32, (R, nb * RET_DV), 0) >> shift
    wide_c = lax.broadcasted_iota(jnp.int32, (R, nb * RET_DV), 1) >> (RET_DV.bit_length() - 1)
    own = wide_r == wide_c
    zpad = jnp.zeros((P - R, RET_DK), BF16)
    zpad_w = jnp.zeros((P - R, nb * RET_DV), BF16)
    cs, sn = cos_ref[...], sin_ref[...]
    for h in range(RET_HEADS):
        lg = lg_ref[h]
        sl = slice(h * RET_DK, (h + 1) * RET_DK)
        dmat = jnp.where(same, jnp.exp(dt * lg), 0.0)
        q_dec = jnp.exp((t_row + 1.0) * lg)
        k_dec = jnp.exp((steps - 1.0 - t_row) * lg)
        s_dec = jnp.exp(jnp.full((RET_DK, RET_DV), float(steps), F32) * lg)
        q = _rotary(q_ref[:, sl], cs, sn)
        k = _rotary(k_ref[:, sl], cs, sn) * scale
        v = v_ref[:, sl]
        qb = q.astype(BF16)
        k_pad = jnp.concatenate([k.astype(BF16), zpad], axis=0)
        v_pad = jnp.concatenate([v.astype(BF16), zpad], axis=0)
        scores = _mm_nt(qb, k_pad) * dmat
        intra = _mm(scores.astype(BF16), v_pad)
        s_cat = jnp.concatenate([st_ref[0, b, h] for b in range(nb)], axis=1)
        wide = jnp.where(own, _mm(qb, s_cat.astype(BF16)), 0.0)
        cross = wide[:, 0:RET_DV]
        for b in range(1, nb):
            cross = cross + wide[:, b * RET_DV:(b + 1) * RET_DV]
        o = intra + cross * q_dec
        o_ref[:, sl] = _group_norm_gate(o, gn_ref[:, sl], g_ref[:, sl]).astype(BF16)
        kd_pad = jnp.concatenate([(k * k_dec).astype(BF16), zpad], axis=0)
        v_wide = jnp.where(own, jnp.concatenate([v] * nb, axis=1), 0.0).astype(BF16)
        upd = _mm_tn(kd_pad, jnp.concatenate([v_wide, zpad_w], axis=0))
        for b in range(nb):
            so_ref[0, b, h] = s_dec * st_ref[0, b, h] + upd[:, b * RET_DV:(b + 1) * RET_DV]
    for later in range(1, so_ref.shape[0]):
        so_ref[later] = jnp.zeros(so_ref.shape[1:], F32)


def _ret_sample_kernel_stacked(*refs, nb, steps):
    _ret_sample_kernel(*refs[:9], *refs[10:], nb=nb, steps=steps)


def _ret_sample(ret_tok, lg, cos, sin, gn, states, steps, layer, prev_out):
    n = ret_tok.shape[0]
    batch = n // steps
    nb = 8
    R = nb * steps
    col = lambda j: pl.BlockSpec((R, RET_W), lambda i, j=j: (i, j))
    st_spec = pl.BlockSpec((1, nb, RET_HEADS, RET_DK, RET_DV), lambda i: (layer, i, 0, 0, 0))
    in_specs = [
        pl.BlockSpec(memory_space=pltpu.SMEM),
        col(0), col(1), col(2), col(3),
        _resident((R, RET_DK)), _resident((R, RET_DK)), _resident((1, RET_W)),
        st_spec,
    ]
    args = [lg, ret_tok, ret_tok, ret_tok, ret_tok, cos, sin, gn, states]
    body, aliases = _ret_sample_kernel, {}
    if prev_out is None:
        assert layer == 0
        out_spec = pl.BlockSpec((states.shape[0], nb, RET_HEADS, RET_DK, RET_DV), lambda i: (0, i, 0, 0, 0))
    else:
        out_spec = st_spec
        body, aliases = _ret_sample_kernel_stacked, {9: 1}
        in_specs.append(pl.BlockSpec(memory_space=pl.ANY))
        args.append(prev_out)
    return pl.pallas_call(
        functools.partial(body, nb=nb, steps=steps),
        grid=(batch // nb,),
        in_specs=in_specs,
        out_specs=[pl.BlockSpec((R, RET_W), lambda i: (i, 0)), out_spec],
        out_shape=[
            jax.ShapeDtypeStruct((n, RET_W), BF16),
            jax.ShapeDtypeStruct(states.shape, F32),
        ],
        input_output_aliases=aliases,
        compiler_params=_cparams("parallel"),
        name="ret_sample",
    )(*args)


def _sb_prompt_kernel(bias_ref, q_ref, k_ref, v_ref, o_ref, kb_ref, vb_ref):
    head = pl.program_id(1)
    bias = bias_ref[head]
    n = SB_TILE
    seq = q_ref.shape[0]
    n_blocks = seq // n
    scale = SB_DH ** -0.5
    kb_ref[...] = k_ref[0, pl.ds(head, seq, stride=SB_HEADS), :].astype(BF16)
    vb_ref[...] = v_ref[0, pl.ds(head, seq, stride=SB_HEADS), :].astype(BF16)
    tri = _strict_lower(n)
    tri2 = jnp.concatenate([tri, tri], axis=0)
    row = lax.broadcasted_iota(jnp.int32, (n, n), 0)
    col = lax.broadcasted_iota(jnp.int32, (n, n), 1)
    causal = col < row

    tiles = [(qi, c) for qi in range(n_blocks) for c in range(qi, -1, -1)]

    scale2, bias2 = scale * LOG2E, bias * LOG2E

    def logits(qi, c):
        return _mm_nt(q_ref[qi * n:(qi + 1) * n, :], kb_ref[c * n:(c + 1) * n, :]) * scale2 + bias2

    def keep(qi, c, z):
        sp = jnp.maximum(z, 0.0) + jnp.log(1.0 + jnp.exp2(_neg_abs(z))) * LOG2E
        drop = jnp.where(causal, sp, 0.0) if c == qi else sp
        return z - sp - _suffix_sum(drop, tri2), jnp.sum(drop, axis=1, keepdims=True)

    acc = run = None

    def weigh(qi, c, e, tot):
        nonlocal acc, run
        a = jnp.exp2(e if c == qi else e - run)
        if c == qi:
            a = jnp.where(causal, a, 0.0)
        term = _mm(a.astype(BF16), vb_ref[c * n:(c + 1) * n, :])
        acc = term if c == qi else acc + term
        run = tot if c == qi else run + tot
        if c == 0:
            o_ref[qi * n:(qi + 1) * n, :] = acc.astype(BF16)

    zs, es = {}, {}
    for i in range(len(tiles) + 2):
        if i < len(tiles):
            zs[i] = logits(*tiles[i])
        if 1 <= i <= len(tiles):
            es[i - 1] = keep(*tiles[i - 1], zs.pop(i - 1))
        if i >= 2:
            weigh(*tiles[i - 2], *es.pop(i - 2))


def _sb_prompt(sbq, sbk, sbv, bias, batch, seq, layer):
    spec = pl.BlockSpec((seq, SB_DH), lambda b, h: (b, h))
    kv_spec = pl.BlockSpec((1, seq * SB_HEADS, SB_DH), lambda b, h: (layer, b, 0))
    return pl.pallas_call(
        _sb_prompt_kernel,
        grid=(batch, SB_HEADS),
        in_specs=[pl.BlockSpec(memory_space=pltpu.SMEM), spec, kv_spec, kv_spec],
        out_specs=spec,
        out_shape=jax.ShapeDtypeStruct((batch * seq, SB_W), BF16),
        scratch_shapes=[pltpu.VMEM((seq, SB_DH), BF16), pltpu.VMEM((seq, SB_DH), BF16)],
        compiler_params=_cparams("parallel", "parallel"),
        name="sb_prompt",
    )(bias, sbq, sbk, sbv)


def _sb_sample_kernel(pt_ref, bias_ref, q_ref, kn_ref, vn_ref, k_hbm, v_hbm, o_ref, kbuf, vbuf, newk, newv, sem,
                      *, layer, n_pages, steps):
    b = pl.program_id(0)
    slot = b % 2
    H = SB_HEADS
    hbits = H.bit_length() - 1
    nq = steps * H
    page_rows = k_hbm.shape[2]
    n = SB_TILE
    n_chunks = n_pages * page_rows // (H * n)
    scale = SB_DH ** -0.5

    def copies(bb, sl):
        out = []
        for p in range(n_pages):
            pg = pt_ref[bb * n_pages + p]
            rows = pl.ds(p * page_rows, page_rows)
            out.append(pltpu.make_async_copy(k_hbm.at[layer, pg], kbuf.at[sl, rows, :], sem.at[0, sl]))
            out.append(pltpu.make_async_copy(v_hbm.at[layer, pg], vbuf.at[sl, rows, :], sem.at[1, sl]))
        return out

    def tile(buf, g):
        heads = [buf[slot, pl.ds(g * n * H + h, n, stride=H), :] for h in range(H)]
        return jnp.concatenate(heads, axis=1).astype(BF16)

    @pl.when(b == 0)
    def _():
        for c in copies(0, 0):
            c.start()

    @pl.when(b + 1 < pl.num_programs(0))
    def _():
        for c in copies(b + 1, 1 - slot):
            c.start()

    rw = lax.broadcasted_iota(jnp.int32, (nq, SB_W), 0)
    cw = lax.broadcasted_iota(jnp.int32, (nq, SB_W), 1)
    own = (rw & (H - 1)) == (cw >> (SB_DH.bit_length() - 1))
    q = q_ref[0]
    wq = jnp.where(own, jnp.concatenate([q] * H, axis=1), jnp.zeros((nq, SB_W), BF16))
    rn = lax.broadcasted_iota(jnp.int32, (nq, n), 0)
    cn = lax.broadcasted_iota(jnp.int32, (nq, n), 1)
    rh = rn & (H - 1)
    bias = jnp.where(rh == 0, bias_ref[0], jnp.where(rh == 1, bias_ref[1],
                     jnp.where(rh == 2, bias_ref[2], bias_ref[3])))
    tri = _strict_lower(n)

    @pl.when(b == 0)
    def _():
        newk[...] = jnp.zeros(newk.shape, F32)
        newv[...] = jnp.zeros(newv.shape, F32)

    newk[0:steps, :] = kn_ref[0]
    newv[0:steps, :] = vn_ref[0]
    valid = cn < (rn >> hbits)

    def scores(k_rows):
        z = _mm_nt(wq, k_rows) * scale + bias
        return z, _softplus(z)

    z, sp = scores(newk[...].astype(BF16))
    parts = [(z, sp, jnp.where(valid, -sp, 0.0), None)]

    for c in copies(b, slot):
        c.wait()

    for g in range(n_chunks - 1, -1, -1):
        z, sp = scores(tile(kbuf, g))
        parts.append((z, sp, -sp, g))

    split = []
    for _, _, lk, _ in parts:
        hi = lk.astype(BF16)
        split += [hi, (lk - hi.astype(F32)).astype(BF16)]
    suffix = _mm(jnp.concatenate(split, axis=0), tri)

    res = run = None
    for i, (z, sp, lk, g) in enumerate(parts):
        e = z - sp + suffix[2 * i * nq:(2 * i + 1) * nq] + suffix[(2 * i + 1) * nq:(2 * i + 2) * nq]
        if run is not None:
            e = e + run
        a = jnp.exp(e)
        if g is None:
            a = jnp.where(valid, a, 0.0)
            v_rows = newv[...].astype(BF16)
        else:
            v_rows = tile(vbuf, g)
        term = _mm(a.astype(BF16), v_rows)
        tot = jnp.sum(lk, axis=1, keepdims=True)
        res = term if res is None else res + term
        run = tot if run is None else run + tot

    out = jnp.where(own, res, 0.0)
    acc = out[:, 0:SB_DH]
    for h in range(1, H):
        acc = acc + out[:, h * SB_DH:(h + 1) * SB_DH]
    o_ref[0] = acc.astype(BF16)


def _sb_sample(page_table, bias, q16, k_new, v_new, cache_k, cache_v, layer, steps):
    batch, n_pages = page_table.shape
    nq = steps * SB_HEADS
    depth, n_phys, page_size = cache_k.shape[:3]
    cache_k = cache_k.reshape(depth, n_phys, page_size * SB_HEADS, SB_DH)
    cache_v = cache_v.reshape(depth, n_phys, page_size * SB_HEADS, SB_DH)
    past_rows = n_pages * page_size * SB_HEADS
    row = pl.BlockSpec((1, nq, SB_DH), lambda b, pt: (b, 0, 0))
    new = pl.BlockSpec((1, steps, SB_W), lambda b, pt: (b, 0, 0))
    return pl.pallas_call(
        functools.partial(_sb_sample_kernel, layer=layer, n_pages=n_pages, steps=steps),
        grid_spec=pltpu.PrefetchScalarGridSpec(
            num_scalar_prefetch=1,
            grid=(batch,),
            in_specs=[
                pl.BlockSpec(memory_space=pltpu.SMEM),
                row, new, new,
                pl.BlockSpec(memory_space=pl.ANY),
                pl.BlockSpec(memory_space=pl.ANY),
            ],
            out_specs=row,
            scratch_shapes=[
                pltpu.VMEM((2, past_rows, SB_DH), F32),
                pltpu.VMEM((2, past_rows, SB_DH), F32),
                pltpu.VMEM((SB_TILE, SB_W), F32),
                pltpu.VMEM((SB_TILE, SB_W), F32),
                pltpu.SemaphoreType.DMA((2, 2)),
            ],
        ),
        out_shape=jax.ShapeDtypeStruct((batch, nq, SB_DH), BF16),
        compiler_params=_cparams("arbitrary"),
        name="sb_sample",
    )(page_table.reshape(-1), bias, q16, k_new, v_new, cache_k, cache_v)


def _lru_kernel(x_ref, hist_ref, h0_ref, cw_ref, cb_ref, wa_ref, ba_ref, wx_ref, bx_ref, lam_ref,
                h_ref, hl_ref, tail, hc, a_s, b_s, *xt, sub):
    token_major = bool(xt)
    taps = LRU_CONV
    if token_major:
        rows, tc, width = x_ref.shape
        xt[0][...] = jnp.swapaxes(x_ref[...], 0, 1)
        x_ref = xt[0]
    else:
        tc, rows, width = x_ref.shape

    @pl.when(pl.program_id(0) == 0)
    def _():
        tail[...] = hist_ref[...]
        hc[...] = h0_ref[...]

    lam = lam_ref[...]
    log_lam = -(jnp.maximum(-lam, 0.0) + jnp.log1p(jnp.exp(-jnp.abs(lam))))
    for j in range(tc // sub):
        t0 = j * sub
        if t0 < taps - 1:
            xe = jnp.concatenate([tail[t0:taps - 1], x_ref[0:t0 + sub]], axis=0)
        else:
            xe = x_ref[t0 - (taps - 1):t0 + sub]
        xc = cb_ref[...] + xe[0:sub] * cw_ref[0]
        for i in range(1, taps):
            xc = xc + xe[i:i + sub] * cw_ref[i]
        xc = xc.reshape(sub * rows, width)
        r_parts, i_parts = [], []
        for n in range(LRU_BLOCKS):
            blk = xc[:, n * LRU_BW:(n + 1) * LRU_BW].astype(BF16)
            r_parts.append(_mm(blk, wa_ref[n]))
            i_parts.append(_mm(blk, wx_ref[n]))
        r = jax.nn.sigmoid(jnp.concatenate(r_parts, axis=1) + ba_ref[...])
        gate_i = jax.nn.sigmoid(jnp.concatenate(i_parts, axis=1) + bx_ref[...])
        log_a = LRU_C * r * log_lam
        a_s[t0:t0 + sub] = jnp.exp(log_a).reshape(sub, rows, width)
        one_minus_a2 = -jnp.tanh(log_a) * (jnp.exp(2.0 * log_a) + 1.0)
        b_s[t0:t0 + sub] = (jnp.sqrt(one_minus_a2) * (gate_i * xc)).reshape(sub, rows, width)
    tail[...] = x_ref[tc - (taps - 1):tc]

    hist_dst = a_s if token_major else h_ref

    def step(t, h):
        h = a_s[t] * h + b_s[t]
        hist_dst[t] = h
        return h

    h = lax.fori_loop(0, tc, step, hc[...], unroll=min(tc, 8))
    hc[...] = h
    hl_ref[...] = h
    if token_major:
        h_ref[...] = jnp.swapaxes(a_s[...], 0, 1).astype(h_ref.dtype)


def _lru(x, hist, h0, cw, cb, wa, ba, wx, bx, lam, tc, token_major):
    if token_major:
        rows, steps, width = x.shape
        blk = pl.BlockSpec((rows, tc, width), lambda i: (0, i, 0))
        h_shape = jax.ShapeDtypeStruct((rows, steps, width), BF16)
    else:
        steps, rows, width = x.shape
        blk = pl.BlockSpec((tc, rows, width), lambda i: (i, 0, 0))
        h_shape = jax.ShapeDtypeStruct((steps, rows, width), F32)
    sub = max(1, LRU_ROW_CHUNK // rows)
    vec = _resident((1, width))
    time_major_buf = pltpu.VMEM((tc, rows, width), F32)
    return pl.pallas_call(
        functools.partial(_lru_kernel, sub=sub),
        grid=(steps // tc,),
        in_specs=[
            blk, _resident((LRU_CONV - 1, rows, width)), _resident((rows, width)),
            _resident((LRU_CONV, 1, width)), vec,
            _resident((LRU_BLOCKS, LRU_BW, LRU_BW)), vec,
            _resident((LRU_BLOCKS, LRU_BW, LRU_BW)), vec, vec,
        ],
        out_specs=[blk, pl.BlockSpec((rows, width), lambda i: (0, 0))],
        out_shape=[h_shape, jax.ShapeDtypeStruct((rows, width), F32)],
        scratch_shapes=[
            pltpu.VMEM((LRU_CONV - 1, rows, width), F32),
            pltpu.VMEM((rows, width), F32),
            time_major_buf, time_major_buf,
        ] + ([time_major_buf] if token_major else []),
        compiler_params=_cparams("arbitrary"),
        name="lru",
    )(x, hist, h0, cw, cb, wa, ba, wx, bx, lam)


def _merge_kernel(x_ref, oret_ref, osb_ref, olru_ref, g_ref, wgl_ref, wr_ref, ws_ref, wl_ref, wo_ref,
                  out_ref, m_ref):
    x = x_ref[...]
    xn = _rms(x, g_ref[...]).astype(BF16)
    branches = ((oret_ref, wr_ref), (osb_ref, ws_ref), (olru_ref, wl_ref))
    cw = 256
    for c in range(D_MODEL // cw):
        cols = slice(c * cw, (c + 1) * cw)
        m = None
        for j, (o_ref, w_ref) in enumerate(branches):
            gate = jax.nn.sigmoid(_mm(xn, wgl_ref[:, j * D_MODEL + c * cw:j * D_MODEL + (c + 1) * cw]))
            term = gate * _mm(o_ref[...], w_ref[:, cols])
            m = term if m is None else m + term
        m_ref[:, cols] = m.astype(BF16)
    out_ref[...] = x + _mm(m_ref[...], wo_ref[...])


def _merge(x, o_ret, o_sb, o_lru, g, wgl, wr, ws, wl, wo):
    n = x.shape[0]
    tm = TOKEN_TILE
    row = lambda width: pl.BlockSpec((tm, width), lambda i: (i, 0))
    return pl.pallas_call(
        _merge_kernel,
        grid=(n // tm,),
        in_specs=[
            row(D_MODEL), row(RET_W), row(SB_W), row(LRU_WIDTH),
            _resident((1, D_MODEL)), _resident((D_MODEL, GATE_COLS)),
            _resident((RET_W, D_MODEL)), _resident((SB_W, D_MODEL)), _resident((LRU_WIDTH, D_MODEL)),
            _resident((D_MODEL, D_MODEL)),
        ],
        out_specs=row(D_MODEL),
        out_shape=jax.ShapeDtypeStruct((n, D_MODEL), F32),
        scratch_shapes=[pltpu.VMEM((tm, D_MODEL), BF16)],
        compiler_params=_cparams("parallel"),
        name="merge",
    )(x, o_ret, o_sb, o_lru, g, wgl, wr, ws, wl, wo)


def _ffn_kernel(x_ref, g_ref, wg_ref, wu_ref, cw_ref, cb_ref, wd_ref, hist_ref, nf_ref, out_ref, st_ref,
                carry, *, shift, tiles_per_seq, final_norm):
    tm = x_ref.shape[0]
    hist_rows = carry.shape[0]
    i = pl.program_id(0)

    @pl.when(i % tiles_per_seq == 0)
    def _():
        carry[...] = hist_ref[...]

    x = x_ref[...]
    xn = _rms(x, g_ref[...]).astype(BF16)
    S = V7X_SUBLANES
    n_chunks = D_FF // FFN_COL_CHUNK
    col_slice = lambda c: slice(c * FFN_COL_CHUNK, (c + 1) * FFN_COL_CHUNK)

    def up(c):
        return _mm(xn, wg_ref[:, col_slice(c)]), _mm(xn, wu_ref[:, col_slice(c)])

    def act(c, g, u):
        cols = col_slice(c)
        hist = carry[:, cols]
        if shift == 1:
            ext = jnp.concatenate([hist, g[0:S]], axis=0)
            p1 = jnp.concatenate([pltpu.roll(ext, 1, 0)[S:2 * S], pltpu.roll(g, 1, 0)[S:]], axis=0)
            p2 = jnp.concatenate([pltpu.roll(ext, 2, 0)[S:2 * S], pltpu.roll(g, 2, 0)[S:]], axis=0)
        else:
            p1 = jnp.concatenate([hist[shift:2 * shift], g[0:tm - shift]], axis=0)
            p2 = jnp.concatenate([hist[0:2 * shift], g[0:tm - 2 * shift]], axis=0)
        carry[:, cols] = g[tm - hist_rows:tm]
        gc = cb_ref[:, cols] + p2 * cw_ref[0, :, cols] + p1 * cw_ref[1, :, cols] + g * cw_ref[2, :, cols]
        return (jax.nn.gelu(gc) * u).astype(BF16)

    acc = None
    ups, acts = {}, {}
    for c in range(n_chunks + 2):
        if c < n_chunks:
            ups[c] = up(c)
        if 1 <= c <= n_chunks:
            acts[c - 1] = act(c - 1, *ups.pop(c - 1))
        if c >= 2:
            term = _mm(acts.pop(c - 2), wd_ref[col_slice(c - 2), :])
            acc = term if acc is None else acc + term
    st_ref[...] = carry[hist_rows - 2 * shift:hist_rows, :].reshape(st_ref.shape)
    y = x + acc
    if final_norm:
        y = _rms(y, nf_ref[...])
    out_ref[...] = y


def _ffn(x, g, wg, wu, cw, cb, wd, hist, nf, shift, tiles_per_seq, final_norm):
    n = x.shape[0]
    tm = TOKEN_TILE
    n_seq = n // (tm * tiles_per_seq)
    hist_rows = hist.shape[0]
    row = pl.BlockSpec((tm, D_MODEL), lambda i: (i, 0))
    return pl.pallas_call(
        functools.partial(_ffn_kernel, shift=shift, tiles_per_seq=tiles_per_seq, final_norm=final_norm),
        grid=(n // tm,),
        in_specs=[
            row, _resident((1, D_MODEL)),
            _resident((D_MODEL, D_FF)), _resident((D_MODEL, D_FF)),
            _resident((FFN_CONV, 1, D_FF)), _resident((1, D_FF)),
            _resident((D_FF, D_MODEL)),
            _resident((hist_rows, D_FF)), _resident((1, D_MODEL)),
        ],
        out_specs=[row, pl.BlockSpec((1, 2 * shift, D_FF), lambda i: (i // tiles_per_seq, 0, 0))],
        out_shape=[
            jax.ShapeDtypeStruct((n, D_MODEL), F32),
            jax.ShapeDtypeStruct((n_seq, 2 * shift, D_FF), F32),
        ],
        scratch_shapes=[pltpu.VMEM((hist_rows, D_FF), F32)],
        compiler_params=_cparams("arbitrary"),
        name="ffn",
    )(x, g, wg, wu, cw, cb, wd, hist, nf)


def _rot_tables(pos):
    half = RET_DK // 2
    inv = ROPE_BASE ** (-jnp.arange(half, dtype=F32) / half)
    ang = pos[:, None] * inv[None, :]
    c, s = jnp.cos(ang), jnp.sin(ang)
    return jnp.concatenate([c, c], axis=-1), jnp.concatenate([-s, s], axis=-1)


def _layer_weights(l, p):
    w_in = p["w_in"][l]
    return dict(
        norm1=p["norm1"][l][None, :],
        w_proj=w_in[:, :PROJ_COLS].astype(BF16),
        w_gate=w_in[:, PROJ_COLS:].astype(BF16),
        ret_gn=p["ret_gn"][l][None, :],
        sb_bias=p["sb_bias"][l],
        lru_cw=p["lru_conv_w"][l][:, None, :],
        lru_cb=p["lru_conv_b"][l][None, :],
        lru_wa=p["lru_w_a"][l].astype(BF16),
        lru_ba=p["lru_b_a"][l][None, :],
        lru_wx=p["lru_w_x"][l].astype(BF16),
        lru_bx=p["lru_b_x"][l][None, :],
        lru_lam=p["lru_lambda"][l][None, :],
        w_br_ret=p["w_br_ret"][l].astype(BF16),
        w_br_sb=p["w_br_sb"][l].astype(BF16),
        w_br_lru=p["w_br_lru"][l].astype(BF16),
        w_out=p["w_out"][l].astype(BF16),
        norm2=p["norm2"][l][None, :],
        w_ffn_gate=p["w_ffn_gate"][l].astype(BF16),
        w_ffn_up=p["w_ffn_up"][l].astype(BF16),
        ffn_cw=p["ffn_conv_w"][l][:, None, :],
        ffn_cb=p["ffn_conv_b"][l][None, :],
        w_ffn_down=p["w_ffn_down"][l].astype(BF16),
        norm_f=p["norm_f"][None, :],
    )


def _mix_and_ffn(x, o_ret, o_sb, o_lru, lw, ffn_hist, shift, tiles_per_seq, final_norm):
    x1 = _merge(x, o_ret, o_sb, o_lru, lw["norm1"], lw["w_gate"], lw["w_br_ret"], lw["w_br_sb"],
                lw["w_br_lru"], lw["w_out"])
    return _ffn(x1, lw["norm2"], lw["w_ffn_gate"], lw["w_ffn_up"], lw["ffn_cw"], lw["ffn_cb"],
                lw["w_ffn_down"], ffn_hist, lw["norm_f"], shift, tiles_per_seq, final_norm)


def _prompt_layer(x, lw, lg, cos, sin, batch, seq, layer, depth, kv_stack, final_norm):
    ret, sbq, sbk, sbv, lx = _proj(x, lw["norm1"], lw["w_proj"], stacked=(layer, depth) + kv_stack)
    o_ret, s_new = _ret_prompt(ret, lg, cos, sin, lw["ret_gn"], batch, seq)
    o_sb = _sb_prompt(sbq, sbk, sbv, lw["sb_bias"], batch, seq, layer)
    lx3 = lx.reshape(batch, seq, LRU_WIDTH)
    h, h_last = _lru(
        lx3, jnp.zeros((LRU_CONV - 1, batch, LRU_WIDTH), F32), jnp.zeros((batch, LRU_WIDTH), F32),
        lw["lru_cw"], lw["lru_cb"], lw["lru_wa"], lw["lru_ba"], lw["lru_wx"], lw["lru_bx"], lw["lru_lam"],
        tc=256, token_major=True)
    o_lru = h.reshape(batch * seq, LRU_WIDTH)
    x2, ffn_state = _mix_and_ffn(x, o_ret, o_sb, o_lru, lw, jnp.zeros((V7X_SUBLANES, D_FF), F32),
                                 shift=1, tiles_per_seq=seq // TOKEN_TILE, final_norm=final_norm)
    states = (s_new, h_last, lx3[:, seq - (LRU_CONV - 1):, :], ffn_state)
    return x2, (sbk, sbv), states


def _sample_layer(x, lw, lg, cos, sin, batch, steps, layer, page_table, cache_k, cache_v,
                  state_ret, ret_out, state_lru_h, state_lru_conv, state_ffn_conv, final_norm):
    to_tok = lambda a: a.reshape(steps, batch, -1).transpose(1, 0, 2).reshape(batch * steps, -1)
    to_tm = lambda a: a.reshape(batch, steps, -1).transpose(1, 0, 2).reshape(steps * batch, -1)
    heads = lambda a: a.reshape(steps, batch, SB_HEADS, SB_DH).transpose(1, 0, 2, 3)
    ret, sbq, sbk, sbv, lx = _proj(x, lw["norm1"], lw["w_proj"])
    o_ret_tok, ret_out = _ret_sample(to_tok(ret), lg, cos, sin, lw["ret_gn"], state_ret, steps, layer, ret_out)
    o_ret = to_tm(o_ret_tok)
    nq = steps * SB_HEADS
    k_new, v_new = heads(sbk), heads(sbv)
    o16 = _sb_sample(page_table, lw["sb_bias"], heads(sbq).reshape(batch, nq, SB_DH),
                     k_new.reshape(batch, steps, SB_W), v_new.reshape(batch, steps, SB_W),
                     cache_k, cache_v, layer, steps)
    o_sb = to_tm(o16.reshape(batch * steps, SB_W))
    lx3 = lx.reshape(steps, batch, LRU_WIDTH)
    h_tm, h_last = _lru(
        lx3, state_lru_conv.transpose(1, 0, 2), state_lru_h,
        lw["lru_cw"], lw["lru_cb"], lw["lru_wa"], lw["lru_ba"], lw["lru_wx"], lw["lru_bx"], lw["lru_lam"],
        tc=steps, token_major=False)
    o_lru = h_tm.reshape(steps * batch, LRU_WIDTH).astype(BF16)
    ffn_hist = state_ffn_conv.transpose(1, 0, 2).reshape((FFN_CONV - 1) * batch, D_FF)
    x2, ffn_state = _mix_and_ffn(x, o_ret, o_sb, o_lru, lw, ffn_hist, shift=batch, tiles_per_seq=1,
                                 final_norm=final_norm)
    states = (
        k_new, v_new, h_last,
        lx3[steps - (LRU_CONV - 1):].transpose(1, 0, 2),
        ffn_state.reshape(FFN_CONV - 1, batch, D_FF).transpose(1, 0, 2),
    )
    return x2, ret_out, states


def kernel(x_prompt, x_sample, cache_sb_k, cache_sb_v, state_ret, state_lru_h, state_lru_conv, state_ffn_conv,
           page_table, norm1, w_in, ret_gn, sb_bias, lru_conv_w, lru_conv_b, lru_w_a, lru_b_a, lru_w_x, lru_b_x,
           lru_lambda, w_br_ret, w_br_sb, w_br_lru, w_out, norm2, w_ffn_gate, w_ffn_up, ffn_conv_w, ffn_conv_b,
           w_ffn_down, norm_f):
    params = dict(norm1=norm1, w_in=w_in, ret_gn=ret_gn, sb_bias=sb_bias, lru_conv_w=lru_conv_w,
                  lru_conv_b=lru_conv_b, lru_w_a=lru_w_a, lru_b_a=lru_b_a, lru_w_x=lru_w_x, lru_b_x=lru_b_x,
                  lru_lambda=lru_lambda, w_br_ret=w_br_ret, w_br_sb=w_br_sb, w_br_lru=w_br_lru, w_out=w_out,
                  norm2=norm2, w_ffn_gate=w_ffn_gate, w_ffn_up=w_ffn_up, ffn_conv_w=ffn_conv_w,
                  ffn_conv_b=ffn_conv_b, w_ffn_down=w_ffn_down, norm_f=norm_f)
    depth = w_in.shape[0]
    batch, seq, _ = x_prompt.shape
    dec_batch, steps, _ = x_sample.shape
    past_len = page_table.shape[1] * cache_sb_k.shape[2]
    assert seq % TOKEN_TILE == 0 and dec_batch * steps == TOKEN_TILE and steps & (steps - 1) == 0
    assert steps >= LRU_CONV - 1 and steps >= FFN_CONV - 1

    lg = jnp.log(1.0 - 2.0 ** (-5.0 - jnp.arange(RET_HEADS, dtype=F32)))
    cos_p, sin_p = _rot_tables(jnp.arange(seq, dtype=F32))
    cos_s, sin_s = _rot_tables(past_len + jnp.arange(steps, dtype=F32))
    nb = 8
    cos_s, sin_s = jnp.tile(cos_s, (nb, 1)), jnp.tile(sin_s, (nb, 1))

    xp = x_prompt.reshape(batch * seq, D_MODEL)
    xs = x_sample.transpose(1, 0, 2).reshape(steps * dec_batch, D_MODEL)
    st_p, st_s = [], []
    kv_stack = (None, None)
    ret_sample = None
    for l in range(depth):
        lw = _layer_weights(l, params)
        last = l == depth - 1
        xp, kv_stack, st = _prompt_layer(xp, lw, lg, cos_p, sin_p, batch, seq, l, depth, kv_stack, last)
        st_p.append(st)
        xs, ret_sample, st = _sample_layer(xs, lw, lg, cos_s, sin_s, dec_batch, steps, l, page_table, cache_sb_k,
                                           cache_sb_v, state_ret, ret_sample, state_lru_h[l], state_lru_conv[l],
                                           state_ffn_conv[l], last)
        st_s.append(st)
    y_prompt = xp.reshape(batch, seq, D_MODEL)
    y_sample = xs.reshape(steps, dec_batch, D_MODEL).transpose(1, 0, 2)
    k_prompt, v_prompt = (a.reshape(depth, batch, seq, SB_HEADS, SB_DH) for a in kv_stack)
    stack = lambda sts, j: jnp.stack([s[j] for s in sts], axis=0)
    return (y_prompt, y_sample,
            k_prompt, v_prompt, stack(st_s, 0), stack(st_s, 1),
            stack(st_p, 0), ret_sample,
            stack(st_p, 1), stack(st_s, 2),
            stack(st_p, 2), stack(st_s, 3),
            stack(st_p, 3), stack(st_s, 4))
```

```python
import functools

import jax
import jax.numpy as jnp
from jax import lax
from jax.experimental import pallas as pl
from jax.experimental.pallas import tpu as pltpu

F32 = jnp.float32
BF16 = jnp.bfloat16

D_MODEL = 1024
RET_HEADS = 4
RET_DK = 128
RET_DV = 128
RET_CHUNK = 128
ROPE_BASE = 10000.0
SB_HEADS = 4
SB_DH = 128
SB_TILE = 256
LRU_WIDTH = 512
LRU_BLOCKS = 4
LRU_BW = LRU_WIDTH // LRU_BLOCKS
LRU_CONV = 4
LRU_C = 8.0
N_BRANCH = 3
D_FF = 2816
FFN_CONV = 3
EPS = 1e-6

RET_W = RET_HEADS * RET_DK
SB_W = SB_HEADS * SB_DH
PROJ_COLS = 4 * RET_W + 3 * SB_W + LRU_WIDTH

V7X_LANES = 128
V7X_SUBLANES = 8
V7X_VMEM_LIMIT_BYTES = 60000 * 1024

TOKEN_TILE = 512
FFN_COL_CHUNK = 256
LRU_ROW_CHUNK = 256


def _cparams(*sem):
    return pltpu.CompilerParams(dimension_semantics=sem, vmem_limit_bytes=V7X_VMEM_LIMIT_BYTES)


def _resident(shape):
    zeros = (0,) * len(shape)
    return pl.BlockSpec(shape, lambda *_: zeros, pipeline_mode=pl.Buffered(1))


def _layer_block(w, layer):
    zeros = (0,) * (w.ndim - 1)
    return pl.BlockSpec((1,) + w.shape[1:], lambda *_: (layer,) + zeros, pipeline_mode=pl.Buffered(1))


def _mm(a, b):
    return jnp.dot(a, b, preferred_element_type=F32)


def _mm_nt(a, b):
    return lax.dot_general(a, b, (((1,), (1,)), ((), ())), preferred_element_type=F32)


def _mm_tn(a, b):
    return lax.dot_general(a, b, (((0,), (0,)), ((), ())), preferred_element_type=F32)


def _rms(x, g):
    return x * lax.rsqrt(jnp.mean(x * x, axis=-1, keepdims=True) + EPS) * g


LOG2E = 1.4426950408889634


def _neg_abs(x):
    bits = lax.bitcast_convert_type(x, jnp.uint32) | jnp.uint32(0x80000000)
    return lax.bitcast_convert_type(bits, F32)


def _softplus(z):
    return jnp.maximum(z, 0.0) + jnp.log(1.0 + jnp.exp(-jnp.abs(z)))


def _suffix_sum(lk, tri):
    hi = lk.astype(BF16)
    lo = (lk - hi.astype(F32)).astype(BF16)
    if tri.shape[0] == 2 * lk.shape[1]:
        return _mm(jnp.concatenate([hi, lo], axis=1), tri)
    return _mm(hi, tri) + _mm(lo, tri)


def _strict_lower(n):
    r = lax.broadcasted_iota(jnp.int32, (n, n), 0)
    c = lax.broadcasted_iota(jnp.int32, (n, n), 1)
    return (r > c).astype(BF16)


def _proj_kernel(x_ref, g_ref, w_ref, cos_ref, sin_ref, retq_ref, ret_ref, sbq_ref, sbk_ref, sbv_ref, lx_ref):
    w = lambda lo, width: w_ref[0, :, lo:lo + width]
    xn = _rms(x_ref[...], g_ref[...]).astype(BF16)
    cs, sn = cos_ref[...], sin_ref[...]
    rq = _mm(xn, w(0, RET_W))
    rk = _mm(xn, w(RET_W, RET_W))
    for h in range(RET_HEADS):
        sl = slice(h * RET_DK, (h + 1) * RET_DK)
        retq_ref[:, sl] = _rotary(rq[:, sl], cs, sn).astype(BF16)
        ret_ref[:, sl] = _rotary(rk[:, sl], cs, sn) * (RET_DK ** -0.5)
    for c in range(2, 4):
        ret_ref[:, (c - 1) * RET_W:c * RET_W] = _mm(xn, w(c * RET_W, RET_W))
    o = 4 * RET_W
    sbq_ref[...] = _mm(xn, w(o, SB_W)).astype(BF16)
    lx_ref[...] = _mm(xn, w(o + 3 * SB_W, LRU_WIDTH))
    for ref, lo in ((sbk_ref, o + SB_W), (sbv_ref, o + 2 * SB_W)):
        kv = _mm(xn, w(lo, SB_W))
        if len(ref.shape) == 2:
            ref[...] = kv
        else:
            tm = kv.shape[0]
            for h in range(SB_HEADS):
                ref[0, pl.ds(h, tm, stride=SB_HEADS), :] = kv[:, h * SB_DH:(h + 1) * SB_DH]
            for later in range(1, ref.shape[0]):
                ref[later] = jnp.zeros(ref.shape[1:], F32)


def _proj_kernel_stacked(x_ref, g_ref, w_ref, cos_ref, sin_ref, prev_k, prev_v, *out_refs):
    _proj_kernel(x_ref, g_ref, w_ref, cos_ref, sin_ref, *out_refs)


def _proj(x, g, w_in, layer, cos, sin, stacked=None):
    n = x.shape[0]
    tm = TOKEN_TILE
    row = lambda width: pl.BlockSpec((tm, width), lambda i: (i, 0))
    period = cos.shape[0] // tm
    table = pl.BlockSpec((tm, RET_DK), lambda i: (i % period, 0))
    in_specs = [row(D_MODEL), _resident((1, D_MODEL)), _layer_block(w_in, layer), table, table]
    args = [x, g, w_in, cos, sin]
    body, aliases = _proj_kernel, {}
    if stacked is None:
        kv_spec, kv_shape = row(SB_W), jax.ShapeDtypeStruct((n, SB_W), F32)
    else:
        depth, prev_k, prev_v = stacked
        kv_shape = jax.ShapeDtypeStruct((depth, n * SB_HEADS, SB_DH), F32)
        if prev_k is None:
            assert layer == 0
            kv_spec = pl.BlockSpec((depth, tm * SB_HEADS, SB_DH), lambda i: (0, i, 0))
        else:
            kv_spec = pl.BlockSpec((1, tm * SB_HEADS, SB_DH), lambda i: (layer, i, 0))
            body, aliases = _proj_kernel_stacked, {5: 3, 6: 4}
            in_specs += [pl.BlockSpec(memory_space=pl.ANY)] * 2
            args += [prev_k, prev_v]
    return pl.pallas_call(
        body,
        grid=(n // tm,),
        in_specs=in_specs,
        out_specs=[row(RET_W), row(3 * RET_W), row(SB_W), kv_spec, kv_spec, row(LRU_WIDTH)],
        out_shape=[
            jax.ShapeDtypeStruct((n, RET_W), BF16),
            jax.ShapeDtypeStruct((n, 3 * RET_W), F32),
            jax.ShapeDtypeStruct((n, SB_W), BF16),
            kv_shape, kv_shape,
            jax.ShapeDtypeStruct((n, LRU_WIDTH), F32),
        ],
        input_output_aliases=aliases,
        compiler_params=_cparams("parallel"),
        name="proj",
    )(*args)


def _rotary(x, cs, sn):
    return x * cs + pltpu.roll(x, RET_DK // 2, 1) * sn


def _group_norm_gate(o, gn, gate):
    mu = jnp.mean(o, axis=-1, keepdims=True)
    d = o - mu
    var = jnp.mean(d * d, axis=-1, keepdims=True)
    return d * lax.rsqrt(var + EPS) * gn * (gate * jax.nn.sigmoid(gate))


def _ret_prompt_kernel(lg_ref, q_ref, k_ref, v_ref, g_ref, gn_ref, o_ref, s_ref):
    lg = lg_ref[pl.program_id(1)]
    L = RET_CHUNK
    n_chunks = q_ref.shape[0] // L
    row = lax.broadcasted_iota(jnp.int32, (L, L), 0).astype(F32)
    col = lax.broadcasted_iota(jnp.int32, (L, L), 1).astype(F32)
    diff = row - col
    dmat = jnp.where(diff >= 0, jnp.exp(jnp.maximum(diff, 0.0) * lg), 0.0)
    q_dec = jnp.exp((row + 1.0) * lg)
    k_dec = jnp.exp((L - 1.0 - row) * lg)
    s_dec = jnp.exp(jnp.full((RET_DK, RET_DV), float(L), F32) * lg)
    gn = gn_ref[...]

    s = jnp.zeros((RET_DK, RET_DV), F32)
    for c in range(n_chunks):
        rows = slice(c * L, (c + 1) * L)
        qb, k = q_ref[rows, :], k_ref[rows, :]
        kb, vb = k.astype(BF16), v_ref[rows, :].astype(BF16)
        scores = _mm_nt(qb, kb) * dmat
        o = _mm(scores.astype(BF16), vb) + _mm(qb, s.astype(BF16)) * q_dec
        o_ref[rows, :] = _group_norm_gate(o, gn, g_ref[rows, :]).astype(BF16)
        s = s_dec * s + _mm_tn((k * k_dec).astype(BF16), vb)
    s_ref[0, 0] = s


def _ret_prompt(retq, ret, lg, gn, batch, seq):
    H = RET_HEADS
    col = lambda j: pl.BlockSpec((seq, RET_DK), lambda b, h, j=j: (b, j * H + h))
    return pl.pallas_call(
        _ret_prompt_kernel,
        grid=(batch, H),
        in_specs=[
            pl.BlockSpec(memory_space=pltpu.SMEM),
            col(0), col(0), col(1), col(2),
            pl.BlockSpec((1, RET_DV), lambda b, h: (0, h)),
        ],
        out_specs=[
            pl.BlockSpec((seq, RET_DV), lambda b, h: (b, h)),
            pl.BlockSpec((1, 1, RET_DK, RET_DV), lambda b, h: (b, h, 0, 0)),
        ],
        out_shape=[
            jax.ShapeDtypeStruct((batch * seq, RET_W), BF16),
            jax.ShapeDtypeStruct((batch, H, RET_DK, RET_DV), F32),
        ],
        compiler_params=_cparams("parallel", "parallel"),
        name="ret_prompt",
    )(lg, retq, ret, ret, ret, gn)


def _ret_sample_kernel(lg_ref, q_ref, k_ref, v_ref, g_ref, gn_ref, st_ref, o_ref, so_ref, *, nb, steps):
    R = nb * steps
    P = RET_DK
    shift = steps.bit_length() - 1
    ri = lax.broadcasted_iota(jnp.int32, (R, P), 0)
    ci = lax.broadcasted_iota(jnp.int32, (R, P), 1)
    rb, rt = ri >> shift, ri & (steps - 1)
    cb, ct = ci >> shift, ci & (steps - 1)
    same = (rb == cb) & (ci < R) & (ct <= rt)
    dt = jnp.maximum(rt - ct, 0).astype(F32)
    t_row = rt.astype(F32)
    wide_r = lax.broadcasted_iota(jnp.int32, (R, nb * RET_DV), 0) >> shift
    wide_c = lax.broadcasted_iota(jnp.int32, (R, nb * RET_DV), 1) >> (RET_DV.bit_length() - 1)
    own = wide_r == wide_c
    zpad = jnp.zeros((P - R, RET_DK), BF16)
    zpad_w = jnp.zeros((P - R, nb * RET_DV), BF16)
    for h in range(RET_HEADS):
        lg = lg_ref[h]
        sl = slice(h * RET_DK, (h + 1) * RET_DK)
        dmat = jnp.where(same, jnp.exp(dt * lg), 0.0)
        q_dec = jnp.exp((t_row + 1.0) * lg)
        k_dec = jnp.exp((steps - 1.0 - t_row) * lg)
        s_dec = jnp.exp(jnp.full((RET_DK, RET_DV), float(steps), F32) * lg)
        qb, k, v = q_ref[:, sl], k_ref[:, sl], v_ref[:, sl]
        k_pad = jnp.concatenate([k.astype(BF16), zpad], axis=0)
        v_pad = jnp.concatenate([v.astype(BF16), zpad], axis=0)
        scores = _mm_nt(qb, k_pad) * dmat
        intra = _mm(scores.astype(BF16), v_pad)
        s_cat = jnp.concatenate([st_ref[0, b, h] for b in range(nb)], axis=1)
        wide = jnp.where(own, _mm(qb, s_cat.astype(BF16)), 0.0)
        cross = wide[:, 0:RET_DV]
        for b in range(1, nb):
            cross = cross + wide[:, b * RET_DV:(b + 1) * RET_DV]
        o = intra + cross * q_dec
        o_ref[:, sl] = _group_norm_gate(o, gn_ref[:, sl], g_ref[:, sl]).astype(BF16)
        kd_pad = jnp.concatenate([(k * k_dec).astype(BF16), zpad], axis=0)
        v_wide = jnp.where(own, jnp.concatenate([v] * nb, axis=1), 0.0).astype(BF16)
        upd = _mm_tn(kd_pad, jnp.concatenate([v_wide, zpad_w], axis=0))
        for b in range(nb):
            so_ref[0, b, h] = s_dec * st_ref[0, b, h] + upd[:, b * RET_DV:(b + 1) * RET_DV]
    for later in range(1, so_ref.shape[0]):
        so_ref[later] = jnp.zeros(so_ref.shape[1:], F32)


def _ret_sample_kernel_stacked(*refs, nb, steps):
    _ret_sample_kernel(*refs[:7], *refs[8:], nb=nb, steps=steps)


def _ret_sample(retq_tok, ret_tok, lg, gn, states, steps, layer, prev_out):
    n = ret_tok.shape[0]
    batch = n // steps
    nb = 8
    R = nb * steps
    col = lambda j: pl.BlockSpec((R, RET_W), lambda i, j=j: (i, j))
    st_spec = pl.BlockSpec((1, nb, RET_HEADS, RET_DK, RET_DV), lambda i: (layer, i, 0, 0, 0))
    in_specs = [
        pl.BlockSpec(memory_space=pltpu.SMEM),
        col(0), col(0), col(1), col(2),
        _resident((1, RET_W)),
        st_spec,
    ]
    args = [lg, retq_tok, ret_tok, ret_tok, ret_tok, gn, states]
    body, aliases = _ret_sample_kernel, {}
    if prev_out is None:
        assert layer == 0
        out_spec = pl.BlockSpec((states.shape[0], nb, RET_HEADS, RET_DK, RET_DV), lambda i: (0, i, 0, 0, 0))
    else:
        out_spec = st_spec
        body, aliases = _ret_sample_kernel_stacked, {7: 1}
        in_specs.append(pl.BlockSpec(memory_space=pl.ANY))
        args.append(prev_out)
    return pl.pallas_call(
        functools.partial(body, nb=nb, steps=steps),
        grid=(batch // nb,),
        in_specs=in_specs,
        out_specs=[pl.BlockSpec((R, RET_W), lambda i: (i, 0)), out_spec],
        out_shape=[
            jax.ShapeDtypeStruct((n, RET_W), BF16),
            jax.ShapeDtypeStruct(states.shape, F32),
        ],
        input_output_aliases=aliases,
        compiler_params=_cparams("parallel"),
        name="ret_sample",
    )(*args)


def _sb_prompt_kernel(bias_ref, q_ref, k_ref, v_ref, o_ref, kb_ref, vb_ref):
    head = pl.program_id(1)
    bias = bias_ref[head]
    n = SB_TILE
    seq = q_ref.shape[0]
    n_blocks = seq // n
    scale = SB_DH ** -0.5
    kb_ref[...] = k_ref[0, pl.ds(head, seq, stride=SB_HEADS), :].astype(BF16)
    vb_ref[...] = v_ref[0, pl.ds(head, seq, stride=SB_HEADS), :].astype(BF16)
    tri = _strict_lower(n)
    tri2 = jnp.concatenate([tri, tri], axis=0)
    row = lax.broadcasted_iota(jnp.int32, (n, n), 0)
    col = lax.broadcasted_iota(jnp.int32, (n, n), 1)
    causal = col < row

    tiles = [(qi, c) for qi in range(n_blocks) for c in range(qi, -1, -1)]

    scale2, bias2 = scale * LOG2E, bias * LOG2E

    def logits(qi, c):
        return _mm_nt(q_ref[qi * n:(qi + 1) * n, :], kb_ref[c * n:(c + 1) * n, :]) * scale2 + bias2

    def keep(qi, c, z):
        sp = jnp.maximum(z, 0.0) + jnp.log(1.0 + jnp.exp2(_neg_abs(z))) * LOG2E
        drop = jnp.where(causal, sp, 0.0) if c == qi else sp
        return z - sp - _suffix_sum(drop, tri2), jnp.sum(drop, axis=1, keepdims=True)

    acc = run = None

    def weigh(qi, c, e, tot):
        nonlocal acc, run
        a = jnp.exp2(e if c == qi else e - run)
        if c == qi:
            a = jnp.where(causal, a, 0.0)
        term = _mm(a.astype(BF16), vb_ref[c * n:(c + 1) * n, :])
        acc = term if c == qi else acc + term
        run = tot if c == qi else run + tot
        if c == 0:
            o_ref[qi * n:(qi + 1) * n, :] = acc.astype(BF16)

    zs, es = {}, {}
    for i in range(len(tiles) + 2):
        if i < len(tiles):
            zs[i] = logits(*tiles[i])
        if 1 <= i <= len(tiles):
            es[i - 1] = keep(*tiles[i - 1], zs.pop(i - 1))
        if i >= 2:
            weigh(*tiles[i - 2], *es.pop(i - 2))


def _sb_prompt(sbq, sbk, sbv, bias, batch, seq, layer):
    spec = pl.BlockSpec((seq, SB_DH), lambda b, h: (b, h))
    kv_spec = pl.BlockSpec((1, seq * SB_HEADS, SB_DH), lambda b, h: (layer, b, 0))
    return pl.pallas_call(
        _sb_prompt_kernel,
        grid=(batch, SB_HEADS),
        in_specs=[pl.BlockSpec(memory_space=pltpu.SMEM), spec, kv_spec, kv_spec],
        out_specs=spec,
        out_shape=jax.ShapeDtypeStruct((batch * seq, SB_W), BF16),
        scratch_shapes=[pltpu.VMEM((seq, SB_DH), BF16), pltpu.VMEM((seq, SB_DH), BF16)],
        compiler_params=_cparams("parallel", "parallel"),
        name="sb_prompt",
    )(bias, sbq, sbk, sbv)


def _sb_sample_kernel(pt_ref, bias_ref, q_ref, kn_ref, vn_ref, k_hbm, v_hbm, o_ref, kbuf, vbuf, newk, newv, sem,
                      *, layer, n_pages, steps):
    b = pl.program_id(0)
    slot = b % 2
    H = SB_HEADS
    hbits = H.bit_length() - 1
    nq = steps * H
    page_rows = k_hbm.shape[2]
    n = SB_TILE
    n_chunks = n_pages * page_rows // (H * n)
    scale = SB_DH ** -0.5

    def copies(bb, sl):
        out = []
        for p in range(n_pages):
            pg = pt_ref[bb * n_pages + p]
            rows = pl.ds(p * page_rows, page_rows)
            out.append(pltpu.make_async_copy(k_hbm.at[layer, pg], kbuf.at[sl, rows, :], sem.at[0, sl]))
            out.append(pltpu.make_async_copy(v_hbm.at[layer, pg], vbuf.at[sl, rows, :], sem.at[1, sl]))
        return out

    def tile(buf, g):
        heads = [buf[slot, pl.ds(g * n * H + h, n, stride=H), :] for h in range(H)]
        return jnp.concatenate(heads, axis=1).astype(BF16)

    @pl.when(b == 0)
    def _():
        for c in copies(0, 0):
            c.start()

    @pl.when(b + 1 < pl.num_programs(0))
    def _():
        for c in copies(b + 1, 1 - slot):
            c.start()

    rw = lax.broadcasted_iota(jnp.int32, (nq, SB_W), 0)
    cw = lax.broadcasted_iota(jnp.int32, (nq, SB_W), 1)
    own = (rw & (H - 1)) == (cw >> (SB_DH.bit_length() - 1))
    q = q_ref[0]
    wq = jnp.where(own, jnp.concatenate([q] * H, axis=1), jnp.zeros((nq, SB_W), BF16))
    rn = lax.broadcasted_iota(jnp.int32, (nq, n), 0)
    cn = lax.broadcasted_iota(jnp.int32, (nq, n), 1)
    rh = rn & (H - 1)
    bias = jnp.where(rh == 0, bias_ref[0], jnp.where(rh == 1, bias_ref[1],
                     jnp.where(rh == 2, bias_ref[2], bias_ref[3])))
    tri = _strict_lower(n)

    @pl.when(b == 0)
    def _():
        newk[...] = jnp.zeros(newk.shape, F32)
        newv[...] = jnp.zeros(newv.shape, F32)

    newk[0:steps, :] = kn_ref[0]
    newv[0:steps, :] = vn_ref[0]
    valid = cn < (rn >> hbits)

    def scores(k_rows):
        z = _mm_nt(wq, k_rows) * scale + bias
        return z, _softplus(z)

    z, sp = scores(newk[...].astype(BF16))
    parts = [(z, sp, jnp.where(valid, -sp, 0.0), None)]

    for c in copies(b, slot):
        c.wait()

    for g in range(n_chunks - 1, -1, -1):
        z, sp = scores(tile(kbuf, g))
        parts.append((z, sp, -sp, g))

    split = []
    for _, _, lk, _ in parts:
        hi = lk.astype(BF16)
        split += [hi, (lk - hi.astype(F32)).astype(BF16)]
    suffix = _mm(jnp.concatenate(split, axis=0), tri)

    res = run = None
    for i, (z, sp, lk, g) in enumerate(parts):
        e = z - sp + suffix[2 * i * nq:(2 * i + 1) * nq] + suffix[(2 * i + 1) * nq:(2 * i + 2) * nq]
        if run is not None:
            e = e + run
        a = jnp.exp(e)
        if g is None:
            a = jnp.where(valid, a, 0.0)
            v_rows = newv[...].astype(BF16)
        else:
            v_rows = tile(vbuf, g)
        term = _mm(a.astype(BF16), v_rows)
        tot = jnp.sum(lk, axis=1, keepdims=True)
        res = term if res is None else res + term
        run = tot if run is None else run + tot

    out = jnp.where(own, res, 0.0)
    acc = out[:, 0:SB_DH]
    for h in range(1, H):
        acc = acc + out[:, h * SB_DH:(h + 1) * SB_DH]
    o_ref[0] = acc.astype(BF16)


def _sb_sample(page_table, bias, q16, k_new, v_new, cache_k, cache_v, layer, steps):
    batch, n_pages = page_table.shape
    nq = steps * SB_HEADS
    depth, n_phys, page_size = cache_k.shape[:3]
    cache_k = cache_k.reshape(depth, n_phys, page_size * SB_HEADS, SB_DH)
    cache_v = cache_v.reshape(depth, n_phys, page_size * SB_HEADS, SB_DH)
    past_rows = n_pages * page_size * SB_HEADS
    row = pl.BlockSpec((1, nq, SB_DH), lambda b, pt: (b, 0, 0))
    new = pl.BlockSpec((1, steps, SB_W), lambda b, pt: (b, 0, 0))
    return pl.pallas_call(
        functools.partial(_sb_sample_kernel, layer=layer, n_pages=n_pages, steps=steps),
        grid_spec=pltpu.PrefetchScalarGridSpec(
            num_scalar_prefetch=1,
            grid=(batch,),
            in_specs=[
                pl.BlockSpec(memory_space=pltpu.SMEM),
                row, new, new,
                pl.BlockSpec(memory_space=pl.ANY),
                pl.BlockSpec(memory_space=pl.ANY),
            ],
            out_specs=row,
            scratch_shapes=[
                pltpu.VMEM((2, past_rows, SB_DH), F32),
                pltpu.VMEM((2, past_rows, SB_DH), F32),
                pltpu.VMEM((SB_TILE, SB_W), F32),
                pltpu.VMEM((SB_TILE, SB_W), F32),
                pltpu.SemaphoreType.DMA((2, 2)),
            ],
        ),
        out_shape=jax.ShapeDtypeStruct((batch, nq, SB_DH), BF16),
        compiler_params=_cparams("arbitrary"),
        name="sb_sample",
    )(page_table.reshape(-1), bias, q16, k_new, v_new, cache_k, cache_v)


def _lru_kernel(x_ref, hist_ref, h0_ref, cw_ref, cb_ref, wa_ref, ba_ref, wx_ref, bx_ref, lam_ref,
                h_ref, hl_ref, tail, hc, a_s, b_s, *xt, sub):
    token_major = bool(xt)
    taps = LRU_CONV
    if token_major:
        rows, tc, width = x_ref.shape
        xt[0][...] = jnp.swapaxes(x_ref[...], 0, 1)
        x_ref = xt[0]
    else:
        tc, rows, width = x_ref.shape

    @pl.when(pl.program_id(0) == 0)
    def _():
        tail[...] = hist_ref[...]
        hc[...] = h0_ref[...]

    lam = lam_ref[...]
    log_lam = -(jnp.maximum(-lam, 0.0) + jnp.log1p(jnp.exp(-jnp.abs(lam))))
    for j in range(tc // sub):
        t0 = j * sub
        if t0 < taps - 1:
            xe = jnp.concatenate([tail[t0:taps - 1], x_ref[0:t0 + sub]], axis=0)
        else:
            xe = x_ref[t0 - (taps - 1):t0 + sub]
        xc = cb_ref[...] + xe[0:sub] * cw_ref[0]
        for i in range(1, taps):
            xc = xc + xe[i:i + sub] * cw_ref[i]
        xc = xc.reshape(sub * rows, width)
        r_parts, i_parts = [], []
        for n in range(LRU_BLOCKS):
            blk = xc[:, n * LRU_BW:(n + 1) * LRU_BW].astype(BF16)
            r_parts.append(_mm(blk, wa_ref[n]))
            i_parts.append(_mm(blk, wx_ref[n]))
        r = jax.nn.sigmoid(jnp.concatenate(r_parts, axis=1) + ba_ref[...])
        gate_i = jax.nn.sigmoid(jnp.concatenate(i_parts, axis=1) + bx_ref[...])
        log_a = LRU_C * r * log_lam
        a_s[t0:t0 + sub] = jnp.exp(log_a).reshape(sub, rows, width)
        one_minus_a2 = -jnp.tanh(log_a) * (jnp.exp(2.0 * log_a) + 1.0)
        b_s[t0:t0 + sub] = (jnp.sqrt(one_minus_a2) * (gate_i * xc)).reshape(sub, rows, width)
    tail[...] = x_ref[tc - (taps - 1):tc]

    hist_dst = a_s if token_major else h_ref

    def step(t, h):
        h = a_s[t] * h + b_s[t]
        hist_dst[t] = h
        return h

    h = lax.fori_loop(0, tc, step, hc[...], unroll=min(tc, 8))
    hc[...] = h
    hl_ref[...] = h
    if token_major:
        h_ref[...] = jnp.swapaxes(a_s[...], 0, 1).astype(h_ref.dtype)


def _lru(x, hist, h0, cw, cb, wa, ba, wx, bx, lam, tc, token_major):
    if token_major:
        rows, steps, width = x.shape
        blk = pl.BlockSpec((rows, tc, width), lambda i: (0, i, 0))
        h_shape = jax.ShapeDtypeStruct((rows, steps, width), BF16)
    else:
        steps, rows, width = x.shape
        blk = pl.BlockSpec((tc, rows, width), lambda i: (i, 0, 0))
        h_shape = jax.ShapeDtypeStruct((steps, rows, width), F32)
    sub = max(1, LRU_ROW_CHUNK // rows)
    vec = _resident((1, width))
    time_major_buf = pltpu.VMEM((tc, rows, width), F32)
    return pl.pallas_call(
        functools.partial(_lru_kernel, sub=sub),
        grid=(steps // tc,),
        in_specs=[
            blk, _resident((LRU_CONV - 1, rows, width)), _resident((rows, width)),
            _resident((LRU_CONV, 1, width)), vec,
            _resident((LRU_BLOCKS, LRU_BW, LRU_BW)), vec,
            _resident((LRU_BLOCKS, LRU_BW, LRU_BW)), vec, vec,
        ],
        out_specs=[blk, pl.BlockSpec((rows, width), lambda i: (0, 0))],
        out_shape=[h_shape, jax.ShapeDtypeStruct((rows, width), F32)],
        scratch_shapes=[
            pltpu.VMEM((LRU_CONV - 1, rows, width), F32),
            pltpu.VMEM((rows, width), F32),
            time_major_buf, time_major_buf,
        ] + ([time_major_buf] if token_major else []),
        compiler_params=_cparams("arbitrary"),
        name="lru",
    )(x, hist, h0, cw, cb, wa, ba, wx, bx, lam)


def _merge_kernel(x_ref, oret_ref, osb_ref, olru_ref, g_ref, wgl_ref, wr_ref, ws_ref, wl_ref, wo_ref,
                  out_ref, m_ref):
    x = x_ref[...]
    xn = _rms(x, g_ref[...]).astype(BF16)
    branches = ((oret_ref, wr_ref), (osb_ref, ws_ref), (olru_ref, wl_ref))
    cw = 256
    for c in range(D_MODEL // cw):
        cols = slice(c * cw, (c + 1) * cw)
        m = None
        for j, (o_ref, w_ref) in enumerate(branches):
            lo = PROJ_COLS + j * D_MODEL + c * cw
            gate = jax.nn.sigmoid(_mm(xn, wgl_ref[0, :, lo:lo + cw]))
            term = gate * _mm(o_ref[...], w_ref[0, :, cols])
            m = term if m is None else m + term
        m_ref[:, cols] = m.astype(BF16)
    out_ref[...] = x + _mm(m_ref[...], wo_ref[0])


def _merge(x, o_ret, o_sb, o_lru, g, w_in, wr, ws, wl, wo, layer):
    n = x.shape[0]
    tm = TOKEN_TILE
    row = lambda width: pl.BlockSpec((tm, width), lambda i: (i, 0))
    return pl.pallas_call(
        _merge_kernel,
        grid=(n // tm,),
        in_specs=[
            row(D_MODEL), row(RET_W), row(SB_W), row(LRU_WIDTH),
            _resident((1, D_MODEL)), _layer_block(w_in, layer),
            _layer_block(wr, layer), _layer_block(ws, layer), _layer_block(wl, layer),
            _layer_block(wo, layer),
        ],
        out_specs=row(D_MODEL),
        out_shape=jax.ShapeDtypeStruct((n, D_MODEL), F32),
        scratch_shapes=[pltpu.VMEM((tm, D_MODEL), BF16)],
        compiler_params=_cparams("parallel"),
        name="merge",
    )(x, o_ret, o_sb, o_lru, g, w_in, wr, ws, wl, wo)


def _ffn_kernel(x_ref, g_ref, wg_ref, wu_ref, cw_ref, cb_ref, wd_ref, hist_ref, nf_ref, out_ref, st_ref,
                carry, h_ref, *, shift, tiles_per_seq, final_norm):
    tm = x_ref.shape[0]
    hist_rows = carry.shape[0]
    i = pl.program_id(0)

    @pl.when(i % tiles_per_seq == 0)
    def _():
        carry[...] = hist_ref[...]

    x = x_ref[...]
    xn = _rms(x, g_ref[...]).astype(BF16)
    S = V7X_SUBLANES
    n_chunks = D_FF // FFN_COL_CHUNK
    col_slice = lambda c: slice(c * FFN_COL_CHUNK, (c + 1) * FFN_COL_CHUNK)

    def up(c):
        return _mm(xn, wg_ref[0, :, col_slice(c)]), _mm(xn, wu_ref[0, :, col_slice(c)])

    def act(c, g, u):
        cols = col_slice(c)
        hist = carry[:, cols]
        if shift == 1:
            ext = jnp.concatenate([hist, g[0:S]], axis=0)
            p1 = jnp.concatenate([pltpu.roll(ext, 1, 0)[S:2 * S], pltpu.roll(g, 1, 0)[S:]], axis=0)
            p2 = jnp.concatenate([pltpu.roll(ext, 2, 0)[S:2 * S], pltpu.roll(g, 2, 0)[S:]], axis=0)
        else:
            p1 = jnp.concatenate([hist[shift:2 * shift], g[0:tm - shift]], axis=0)
            p2 = jnp.concatenate([hist[0:2 * shift], g[0:tm - 2 * shift]], axis=0)
        carry[:, cols] = g[tm - hist_rows:tm]
        gc = cb_ref[:, cols] + p2 * cw_ref[0, :, cols] + p1 * cw_ref[1, :, cols] + g * cw_ref[2, :, cols]
        return (jax.nn.gelu(gc) * u).astype(BF16)

    ups = {}
    for c in range(n_chunks + 1):
        if c < n_chunks:
            ups[c] = up(c)
        if c >= 1:
            h_ref[:, col_slice(c - 1)] = act(c - 1, *ups.pop(c - 1))
    st_ref[...] = carry[hist_rows - 2 * shift:hist_rows, :].reshape(st_ref.shape)
    y = x + _mm(h_ref[...], wd_ref[0])
    if final_norm:
        y = _rms(y, nf_ref[...])
    out_ref[...] = y


def _ffn(x, g, wg, wu, cw, cb, wd, hist, nf, layer, shift, tiles_per_seq, final_norm):
    n = x.shape[0]
    tm = TOKEN_TILE
    n_seq = n // (tm * tiles_per_seq)
    hist_rows = hist.shape[0]
    row = pl.BlockSpec((tm, D_MODEL), lambda i: (i, 0))
    return pl.pallas_call(
        functools.partial(_ffn_kernel, shift=shift, tiles_per_seq=tiles_per_seq, final_norm=final_norm),
        grid=(n // tm,),
        in_specs=[
            row, _resident((1, D_MODEL)),
            _layer_block(wg, layer), _layer_block(wu, layer),
            _resident((FFN_CONV, 1, D_FF)), _resident((1, D_FF)),
            _layer_block(wd, layer),
            _resident((hist_rows, D_FF)), _resident((1, D_MODEL)),
        ],
        out_specs=[row, pl.BlockSpec((1, 2 * shift, D_FF), lambda i: (i // tiles_per_seq, 0, 0))],
        out_shape=[
            jax.ShapeDtypeStruct((n, D_MODEL), F32),
            jax.ShapeDtypeStruct((n_seq, 2 * shift, D_FF), F32),
        ],
        scratch_shapes=[pltpu.VMEM((hist_rows, D_FF), F32), pltpu.VMEM((tm, D_FF), BF16)],
        compiler_params=_cparams("arbitrary"),
        name="ffn",
    )(x, g, wg, wu, cw, cb, wd, hist, nf)


def _rot_tables(pos):
    half = RET_DK // 2
    inv = ROPE_BASE ** (-jnp.arange(half, dtype=F32) / half)
    ang = pos[:, None] * inv[None, :]
    c, s = jnp.cos(ang), jnp.sin(ang)
    return jnp.concatenate([c, c], axis=-1), jnp.concatenate([-s, s], axis=-1)


def _layer_weights(l, p):
    return dict(
        layer=l,
        norm1=p["norm1"][l][None, :],
        w_in=p["w_in"],
        ret_gn=p["ret_gn"][l][None, :],
        sb_bias=p["sb_bias"][l],
        lru_cw=p["lru_conv_w"][l][:, None, :],
        lru_cb=p["lru_conv_b"][l][None, :],
        lru_wa=p["lru_w_a"][l].astype(BF16),
        lru_ba=p["lru_b_a"][l][None, :],
        lru_wx=p["lru_w_x"][l].astype(BF16),
        lru_bx=p["lru_b_x"][l][None, :],
        lru_lam=p["lru_lambda"][l][None, :],
        w_br_ret=p["w_br_ret"], w_br_sb=p["w_br_sb"], w_br_lru=p["w_br_lru"], w_out=p["w_out"],
        norm2=p["norm2"][l][None, :],
        w_ffn_gate=p["w_ffn_gate"], w_ffn_up=p["w_ffn_up"],
        ffn_cw=p["ffn_conv_w"][l][:, None, :],
        ffn_cb=p["ffn_conv_b"][l][None, :],
        w_ffn_down=p["w_ffn_down"],
        norm_f=p["norm_f"][None, :],
    )


def _mix_and_ffn(x, o_ret, o_sb, o_lru, lw, ffn_hist, shift, tiles_per_seq, final_norm):
    x1 = _merge(x, o_ret, o_sb, o_lru, lw["norm1"], lw["w_in"], lw["w_br_ret"], lw["w_br_sb"],
                lw["w_br_lru"], lw["w_out"], lw["layer"])
    return _ffn(x1, lw["norm2"], lw["w_ffn_gate"], lw["w_ffn_up"], lw["ffn_cw"], lw["ffn_cb"],
                lw["w_ffn_down"], ffn_hist, lw["norm_f"], lw["layer"], shift, tiles_per_seq, final_norm)


def _prompt_layer(x, lw, lg, cos, sin, batch, seq, layer, depth, kv_stack, final_norm):
    retq, ret, sbq, sbk, sbv, lx = _proj(x, lw["norm1"], lw["w_in"], layer, cos, sin,
                                         stacked=(depth,) + kv_stack)
    o_ret, s_new = _ret_prompt(retq, ret, lg, lw["ret_gn"], batch, seq)
    o_sb = _sb_prompt(sbq, sbk, sbv, lw["sb_bias"], batch, seq, layer)
    lx3 = lx.reshape(batch, seq, LRU_WIDTH)
    h, h_last = _lru(
        lx3, jnp.zeros((LRU_CONV - 1, batch, LRU_WIDTH), F32), jnp.zeros((batch, LRU_WIDTH), F32),
        lw["lru_cw"], lw["lru_cb"], lw["lru_wa"], lw["lru_ba"], lw["lru_wx"], lw["lru_bx"], lw["lru_lam"],
        tc=256, token_major=True)
    o_lru = h.reshape(batch * seq, LRU_WIDTH)
    x2, ffn_state = _mix_and_ffn(x, o_ret, o_sb, o_lru, lw, jnp.zeros((V7X_SUBLANES, D_FF), F32),
                                 shift=1, tiles_per_seq=seq // TOKEN_TILE, final_norm=final_norm)
    states = (s_new, h_last, lx3[:, seq - (LRU_CONV - 1):, :], ffn_state)
    return x2, (sbk, sbv), states


def _sample_layer(x, lw, lg, cos, sin, batch, steps, layer, page_table, cache_k, cache_v,
                  state_ret, ret_out, state_lru_h, state_lru_conv, state_ffn_conv, final_norm):
    to_tok = lambda a: a.reshape(steps, batch, -1).transpose(1, 0, 2).reshape(batch * steps, -1)
    to_tm = lambda a: a.reshape(batch, steps, -1).transpose(1, 0, 2).reshape(steps * batch, -1)
    heads = lambda a: a.reshape(steps, batch, SB_HEADS, SB_DH).transpose(1, 0, 2, 3)
    retq, ret, sbq, sbk, sbv, lx = _proj(x, lw["norm1"], lw["w_in"], layer, cos, sin)
    o_ret_tok, ret_out = _ret_sample(to_tok(retq), to_tok(ret), lg, lw["ret_gn"], state_ret, steps, layer,
                                     ret_out)
    o_ret = to_tm(o_ret_tok)
    nq = steps * SB_HEADS
    k_new, v_new = heads(sbk), heads(sbv)
    o16 = _sb_sample(page_table, lw["sb_bias"], heads(sbq).reshape(batch, nq, SB_DH),
                     k_new.reshape(batch, steps, SB_W), v_new.reshape(batch, steps, SB_W),
                     cache_k, cache_v, layer, steps)
    o_sb = to_tm(o16.reshape(batch * steps, SB_W))
    lx3 = lx.reshape(steps, batch, LRU_WIDTH)
    h_tm, h_last = _lru(
        lx3, state_lru_conv.transpose(1, 0, 2), state_lru_h,
        lw["lru_cw"], lw["lru_cb"], lw["lru_wa"], lw["lru_ba"], lw["lru_wx"], lw["lru_bx"], lw["lru_lam"],
        tc=steps, token_major=False)
    o_lru = h_tm.reshape(steps * batch, LRU_WIDTH).astype(BF16)
    ffn_hist = state_ffn_conv.transpose(1, 0, 2).reshape((FFN_CONV - 1) * batch, D_FF)
    x2, ffn_state = _mix_and_ffn(x, o_ret, o_sb, o_lru, lw, ffn_hist, shift=batch, tiles_per_seq=1,
                                 final_norm=final_norm)
    states = (
        k_new, v_new, h_last,
        lx3[steps - (LRU_CONV - 1):].transpose(1, 0, 2),
        ffn_state.reshape(FFN_CONV - 1, batch, D_FF).transpose(1, 0, 2),
    )
    return x2, ret_out, states


def kernel(x_prompt, x_sample, cache_sb_k, cache_sb_v, state_ret, state_lru_h, state_lru_conv, state_ffn_conv,
           page_table, norm1, w_in, ret_gn, sb_bias, lru_conv_w, lru_conv_b, lru_w_a, lru_b_a, lru_w_x, lru_b_x,
           lru_lambda, w_br_ret, w_br_sb, w_br_lru, w_out, norm2, w_ffn_gate, w_ffn_up, ffn_conv_w, ffn_conv_b,
           w_ffn_down, norm_f):
    bf = lambda w: w.astype(BF16)
    params = dict(norm1=norm1, w_in=bf(w_in), ret_gn=ret_gn, sb_bias=sb_bias, lru_conv_w=lru_conv_w,
                  lru_conv_b=lru_conv_b, lru_w_a=lru_w_a, lru_b_a=lru_b_a, lru_w_x=lru_w_x, lru_b_x=lru_b_x,
                  lru_lambda=lru_lambda, w_br_ret=bf(w_br_ret), w_br_sb=bf(w_br_sb), w_br_lru=bf(w_br_lru),
                  w_out=bf(w_out), norm2=norm2, w_ffn_gate=bf(w_ffn_gate), w_ffn_up=bf(w_ffn_up),
                  ffn_conv_w=ffn_conv_w, ffn_conv_b=ffn_conv_b, w_ffn_down=bf(w_ffn_down), norm_f=norm_f)
    depth = w_in.shape[0]
    batch, seq, _ = x_prompt.shape
    dec_batch, steps, _ = x_sample.shape
    past_len = page_table.shape[1] * cache_sb_k.shape[2]
    assert seq % TOKEN_TILE == 0 and dec_batch * steps == TOKEN_TILE and steps & (steps - 1) == 0
    assert steps >= LRU_CONV - 1 and steps >= FFN_CONV - 1

    lg = jnp.log(1.0 - 2.0 ** (-5.0 - jnp.arange(RET_HEADS, dtype=F32)))
    cos_p, sin_p = _rot_tables(jnp.arange(seq, dtype=F32))
    cos_s, sin_s = _rot_tables(past_len + jnp.repeat(jnp.arange(steps, dtype=F32), dec_batch))

    xp = x_prompt.reshape(batch * seq, D_MODEL)
    xs = x_sample.transpose(1, 0, 2).reshape(steps * dec_batch, D_MODEL)
    st_p, st_s = [], []
    kv_stack = (None, None)
    ret_sample = None
    for l in range(depth):
        lw = _layer_weights(l, params)
        last = l == depth - 1
        xp, kv_stack, st = _prompt_layer(xp, lw, lg, cos_p, sin_p, batch, seq, l, depth, kv_stack, last)
        st_p.append(st)
        xs, ret_sample, st = _sample_layer(xs, lw, lg, cos_s, sin_s, dec_batch, steps, l, page_table, cache_sb_k,
                                           cache_sb_v, state_ret, ret_sample, state_lru_h[l], state_lru_conv[l],
                                           state_ffn_conv[l], last)
        st_s.append(st)
    y_prompt = xp.reshape(batch, seq, D_MODEL)
    y_sample = xs.reshape(steps, dec_batch, D_MODEL).transpose(1, 0, 2)
    k_prompt, v_prompt = (a.reshape(depth, batch, seq, SB_HEADS, SB_DH) for a in kv_stack)
    stack = lambda sts, j: jnp.stack([s[j] for s in sts], axis=0)
    return (y_prompt, y_sample,
            k_prompt, v_prompt, stack(st_s, 0), stack(st_s, 1),
            stack(st_p, 0), ret_sample,
            stack(st_p, 1), stack(st_s, 2),
            stack(st_p, 2), stack(st_s, 3),
            stack(st_p, 3), stack(st_s, 4))
```

```python
import functools

import jax
import jax.numpy as jnp
from jax import lax
from jax.experimental import pallas as pl
from jax.experimental.pallas import tpu as pltpu

F32 = jnp.float32
BF16 = jnp.bfloat16

D_MODEL = 1024
RET_HEADS = 4
RET_DK = 128
RET_DV = 128
RET_CHUNK = 128
ROPE_BASE = 10000.0
SB_HEADS = 4
SB_DH = 128
SB_TILE = 256
SB_Q_ROWS = 256
LRU_WIDTH = 512
LRU_BLOCKS = 4
LRU_BW = LRU_WIDTH // LRU_BLOCKS
LRU_CONV = 4
LRU_C = 8.0
N_BRANCH = 3
D_FF = 2816
FFN_CONV = 3
EPS = 1e-6

RET_W = RET_HEADS * RET_DK
SB_W = SB_HEADS * SB_DH
PROJ_COLS = 4 * RET_W + 3 * SB_W + LRU_WIDTH

V7X_LANES = 128
V7X_SUBLANES = 8
V7X_VMEM_LIMIT_BYTES = 60000 * 1024

TOKEN_TILE = 512
FFN_COL_CHUNK = 256
LRU_ROW_CHUNK = 256


def _cparams(*sem):
    return pltpu.CompilerParams(dimension_semantics=sem, vmem_limit_bytes=V7X_VMEM_LIMIT_BYTES)


def _resident(shape):
    zeros = (0,) * len(shape)
    return pl.BlockSpec(shape, lambda *_: zeros, pipeline_mode=pl.Buffered(1))


def _layer_block(w, layer):
    zeros = (0,) * (w.ndim - 1)
    return pl.BlockSpec((1,) + w.shape[1:], lambda *_: (layer,) + zeros, pipeline_mode=pl.Buffered(1))


def _mm(a, b):
    return jnp.dot(a, b, preferred_element_type=F32)


def _mm_nt(a, b):
    return lax.dot_general(a, b, (((1,), (1,)), ((), ())), preferred_element_type=F32)


def _mm_tn(a, b):
    return lax.dot_general(a, b, (((0,), (0,)), ((), ())), preferred_element_type=F32)


def _rms(x, g):
    return x * lax.rsqrt(jnp.mean(x * x, axis=-1, keepdims=True) + EPS) * g


LOG2E = 1.4426950408889634


def _neg_abs(x):
    bits = lax.bitcast_convert_type(x, jnp.uint32) | jnp.uint32(0x80000000)
    return lax.bitcast_convert_type(bits, F32)


def _softplus(z):
    return jnp.maximum(z, 0.0) + jnp.log(1.0 + jnp.exp(-jnp.abs(z)))


def _suffix_sum(lk, tri):
    hi = lk.astype(BF16)
    lo = (lk - hi.astype(F32)).astype(BF16)
    if tri.shape[0] == 2 * lk.shape[1]:
        return _mm(jnp.concatenate([hi, lo], axis=1), tri)
    return _mm(hi, tri) + _mm(lo, tri)


def _strict_lower(n):
    r = lax.broadcasted_iota(jnp.int32, (n, n), 0)
    c = lax.broadcasted_iota(jnp.int32, (n, n), 1)
    return (r > c).astype(BF16)


def _proj_kernel(x_ref, g_ref, w_ref, cos_ref, sin_ref, retq_ref, ret_ref, sbq_ref, sbk_ref, sbv_ref, lx_ref):
    w = lambda lo, width: w_ref[0, :, lo:lo + width]
    xn = _rms(x_ref[...], g_ref[...]).astype(BF16)
    cs, sn = cos_ref[...], sin_ref[...]
    rq = _mm(xn, w(0, RET_W))
    rk = _mm(xn, w(RET_W, RET_W))
    for h in range(RET_HEADS):
        sl = slice(h * RET_DK, (h + 1) * RET_DK)
        retq_ref[:, sl] = _rotary(rq[:, sl], cs, sn).astype(BF16)
        ret_ref[:, sl] = _rotary(rk[:, sl], cs, sn) * (RET_DK ** -0.5)
    for c in range(2, 4):
        ret_ref[:, (c - 1) * RET_W:c * RET_W] = _mm(xn, w(c * RET_W, RET_W))
    o = 4 * RET_W
    sbq_ref[...] = _mm(xn, w(o, SB_W)).astype(BF16)
    lx_ref[...] = _mm(xn, w(o + 3 * SB_W, LRU_WIDTH))
    for ref, lo in ((sbk_ref, o + SB_W), (sbv_ref, o + 2 * SB_W)):
        kv = _mm(xn, w(lo, SB_W))
        if len(ref.shape) == 2:
            ref[...] = kv
        else:
            tm = kv.shape[0]
            for h in range(SB_HEADS):
                ref[0, pl.ds(h, tm, stride=SB_HEADS), :] = kv[:, h * SB_DH:(h + 1) * SB_DH]
            for later in range(1, ref.shape[0]):
                ref[later] = jnp.zeros(ref.shape[1:], F32)


def _proj_kernel_stacked(x_ref, g_ref, w_ref, cos_ref, sin_ref, prev_k, prev_v, *out_refs):
    _proj_kernel(x_ref, g_ref, w_ref, cos_ref, sin_ref, *out_refs)


def _proj(x, g, w_in, layer, cos, sin, stacked=None):
    n = x.shape[0]
    tm = TOKEN_TILE
    row = lambda width: pl.BlockSpec((tm, width), lambda i: (i, 0))
    period = cos.shape[0] // tm
    table = pl.BlockSpec((tm, RET_DK), lambda i: (i % period, 0))
    in_specs = [row(D_MODEL), _resident((1, D_MODEL)), _layer_block(w_in, layer), table, table]
    args = [x, g, w_in, cos, sin]
    body, aliases = _proj_kernel, {}
    if stacked is None:
        kv_spec, kv_shape = row(SB_W), jax.ShapeDtypeStruct((n, SB_W), F32)
    else:
        depth, prev_k, prev_v = stacked
        kv_shape = jax.ShapeDtypeStruct((depth, n * SB_HEADS, SB_DH), F32)
        if prev_k is None:
            assert layer == 0
            kv_spec = pl.BlockSpec((depth, tm * SB_HEADS, SB_DH), lambda i: (0, i, 0))
        else:
            kv_spec = pl.BlockSpec((1, tm * SB_HEADS, SB_DH), lambda i: (layer, i, 0))
            body, aliases = _proj_kernel_stacked, {5: 3, 6: 4}
            in_specs += [pl.BlockSpec(memory_space=pl.ANY)] * 2
            args += [prev_k, prev_v]
    return pl.pallas_call(
        body,
        grid=(n // tm,),
        in_specs=in_specs,
        out_specs=[row(RET_W), row(3 * RET_W), row(SB_W), kv_spec, kv_spec, row(LRU_WIDTH)],
        out_shape=[
            jax.ShapeDtypeStruct((n, RET_W), BF16),
            jax.ShapeDtypeStruct((n, 3 * RET_W), F32),
            jax.ShapeDtypeStruct((n, SB_W), BF16),
            kv_shape, kv_shape,
            jax.ShapeDtypeStruct((n, LRU_WIDTH), F32),
        ],
        input_output_aliases=aliases,
        compiler_params=_cparams("parallel"),
        name="proj",
    )(*args)


def _rotary(x, cs, sn):
    return x * cs + pltpu.roll(x, RET_DK // 2, 1) * sn


def _group_norm_gate(o, gn, gate):
    mu = jnp.mean(o, axis=-1, keepdims=True)
    d = o - mu
    var = jnp.mean(d * d, axis=-1, keepdims=True)
    return d * lax.rsqrt(var + EPS) * gn * (gate * jax.nn.sigmoid(gate))


def _ret_prompt_kernel(lg_ref, q_ref, k_ref, v_ref, g_ref, gn_ref, o_ref, s_ref):
    lg = lg_ref[pl.program_id(1)]
    L = RET_CHUNK
    n_chunks = q_ref.shape[0] // L
    row = lax.broadcasted_iota(jnp.int32, (L, L), 0).astype(F32)
    col = lax.broadcasted_iota(jnp.int32, (L, L), 1).astype(F32)
    diff = row - col
    dmat = jnp.where(diff >= 0, jnp.exp(jnp.maximum(diff, 0.0) * lg), 0.0)
    q_dec = jnp.exp((row + 1.0) * lg)
    k_dec = jnp.exp((L - 1.0 - row) * lg)
    s_dec = jnp.exp(jnp.full((RET_DK, RET_DV), float(L), F32) * lg)
    gn = gn_ref[...]

    rows_of = lambda c: slice(c * L, (c + 1) * L)

    def local(c):
        rows = rows_of(c)
        qb, k = q_ref[rows, :], k_ref[rows, :]
        vb = v_ref[rows, :].astype(BF16)
        scores = (_mm_nt(qb, k.astype(BF16)) * dmat).astype(BF16)
        return qb, vb, scores, _mm_tn((k * k_dec).astype(BF16), vb)

    s = jnp.zeros((RET_DK, RET_DV), F32)

    def attend(qb, vb, scores, upd):
        nonlocal s
        o = _mm(scores, vb) + _mm(qb, s.astype(BF16)) * q_dec
        s = s_dec * s + upd
        return o

    def emit(c, o):
        rows = rows_of(c)
        o_ref[rows, :] = _group_norm_gate(o, gn, g_ref[rows, :]).astype(BF16)

    locs, outs = {}, {}
    for c in range(n_chunks + 2):
        if c < n_chunks:
            locs[c] = local(c)
        if 1 <= c <= n_chunks:
            outs[c - 1] = attend(*locs.pop(c - 1))
        if c >= 2:
            emit(c - 2, outs.pop(c - 2))
    s_ref[0, 0] = s


def _ret_prompt(retq, ret, lg, gn, batch, seq):
    H = RET_HEADS
    col = lambda j: pl.BlockSpec((seq, RET_DK), lambda b, h, j=j: (b, j * H + h))
    return pl.pallas_call(
        _ret_prompt_kernel,
        grid=(batch, H),
        in_specs=[
            pl.BlockSpec(memory_space=pltpu.SMEM),
            col(0), col(0), col(1), col(2),
            pl.BlockSpec((1, RET_DV), lambda b, h: (0, h)),
        ],
        out_specs=[
            pl.BlockSpec((seq, RET_DV), lambda b, h: (b, h)),
            pl.BlockSpec((1, 1, RET_DK, RET_DV), lambda b, h: (b, h, 0, 0)),
        ],
        out_shape=[
            jax.ShapeDtypeStruct((batch * seq, RET_W), BF16),
            jax.ShapeDtypeStruct((batch, H, RET_DK, RET_DV), F32),
        ],
        compiler_params=_cparams("parallel", "parallel"),
        name="ret_prompt",
    )(lg, retq, ret, ret, ret, gn)


def _ret_sample_kernel(lg_ref, q_ref, k_ref, v_ref, g_ref, gn_ref, st_ref, o_ref, so_ref, *, nb, steps):
    R = nb * steps
    P = RET_DK
    shift = steps.bit_length() - 1
    ri = lax.broadcasted_iota(jnp.int32, (R, P), 0)
    ci = lax.broadcasted_iota(jnp.int32, (R, P), 1)
    rb, rt = ri >> shift, ri & (steps - 1)
    cb, ct = ci >> shift, ci & (steps - 1)
    same = (rb == cb) & (ci < R) & (ct <= rt)
    dt = jnp.maximum(rt - ct, 0).astype(F32)
    t_row = rt.astype(F32)
    wide_r = lax.broadcasted_iota(jnp.int32, (R, nb * RET_DV), 0) >> shift
    wide_c = lax.broadcasted_iota(jnp.int32, (R, nb * RET_DV), 1) >> (RET_DV.bit_length() - 1)
    own = wide_r == wide_c
    zpad = jnp.zeros((P - R, RET_DK), BF16)
    zpad_w = jnp.zeros((P - R, nb * RET_DV), BF16)
    for h in range(RET_HEADS):
        lg = lg_ref[h]
        sl = slice(h * RET_DK, (h + 1) * RET_DK)
        dmat = jnp.where(same, jnp.exp(dt * lg), 0.0)
        q_dec = jnp.exp((t_row + 1.0) * lg)
        k_dec = jnp.exp((steps - 1.0 - t_row) * lg)
        s_dec = jnp.exp(jnp.full((RET_DK, RET_DV), float(steps), F32) * lg)
        qb, k, v = q_ref[:, sl], k_ref[:, sl], v_ref[:, sl]
        k_pad = jnp.concatenate([k.astype(BF16), zpad], axis=0)
        v_pad = jnp.concatenate([v.astype(BF16), zpad], axis=0)
        scores = _mm_nt(qb, k_pad) * dmat
        intra = _mm(scores.astype(BF16), v_pad)
        s_cat = jnp.concatenate([st_ref[0, b, h] for b in range(nb)], axis=1)
        wide = jnp.where(own, _mm(qb, s_cat.astype(BF16)), 0.0)
        cross = wide[:, 0:RET_DV]
        for b in range(1, nb):
            cross = cross + wide[:, b * RET_DV:(b + 1) * RET_DV]
        o = intra + cross * q_dec
        o_ref[:, sl] = _group_norm_gate(o, gn_ref[:, sl], g_ref[:, sl]).astype(BF16)
        kd_pad = jnp.concatenate([(k * k_dec).astype(BF16), zpad], axis=0)
        v_wide = jnp.where(own, jnp.concatenate([v] * nb, axis=1), 0.0).astype(BF16)
        upd = _mm_tn(kd_pad, jnp.concatenate([v_wide, zpad_w], axis=0))
        for b in range(nb):
            so_ref[0, b, h] = s_dec * st_ref[0, b, h] + upd[:, b * RET_DV:(b + 1) * RET_DV]
    for later in range(1, so_ref.shape[0]):
        so_ref[later] = jnp.zeros(so_ref.shape[1:], F32)


def _ret_sample_kernel_stacked(*refs, nb, steps):
    _ret_sample_kernel(*refs[:7], *refs[8:], nb=nb, steps=steps)


def _ret_sample(retq_tok, ret_tok, lg, gn, states, steps, layer, prev_out):
    n = ret_tok.shape[0]
    batch = n // steps
    nb = 8
    R = nb * steps
    col = lambda j: pl.BlockSpec((R, RET_W), lambda i, j=j: (i, j))
    st_spec = pl.BlockSpec((1, nb, RET_HEADS, RET_DK, RET_DV), lambda i: (layer, i, 0, 0, 0))
    in_specs = [
        pl.BlockSpec(memory_space=pltpu.SMEM),
        col(0), col(0), col(1), col(2),
        _resident((1, RET_W)),
        st_spec,
    ]
    args = [lg, retq_tok, ret_tok, ret_tok, ret_tok, gn, states]
    body, aliases = _ret_sample_kernel, {}
    if prev_out is None:
        assert layer == 0
        out_spec = pl.BlockSpec((states.shape[0], nb, RET_HEADS, RET_DK, RET_DV), lambda i: (0, i, 0, 0, 0))
    else:
        out_spec = st_spec
        body, aliases = _ret_sample_kernel_stacked, {7: 1}
        in_specs.append(pl.BlockSpec(memory_space=pl.ANY))
        args.append(prev_out)
    return pl.pallas_call(
        functools.partial(body, nb=nb, steps=steps),
        grid=(batch // nb,),
        in_specs=in_specs,
        out_specs=[pl.BlockSpec((R, RET_W), lambda i: (i, 0)), out_spec],
        out_shape=[
            jax.ShapeDtypeStruct((n, RET_W), BF16),
            jax.ShapeDtypeStruct(states.shape, F32),
        ],
        input_output_aliases=aliases,
        compiler_params=_cparams("parallel"),
        name="ret_sample",
    )(*args)


def _sb_prompt_kernel(bias_ref, q_ref, k_ref, v_ref, o_ref, kb_ref, vb_ref):
    head = pl.program_id(1)
    bias = bias_ref[head]
    n = SB_TILE
    m = SB_Q_ROWS
    seq = q_ref.shape[0]
    scale = SB_DH ** -0.5
    kb_ref[...] = k_ref[0, pl.ds(head, seq, stride=SB_HEADS), :].astype(BF16)
    vb_ref[...] = v_ref[0, pl.ds(head, seq, stride=SB_HEADS), :].astype(BF16)
    tri = _strict_lower(n)
    tri2 = jnp.concatenate([tri, tri], axis=0)
    row = lax.broadcasted_iota(jnp.int32, (m, n), 0)
    col = lax.broadcasted_iota(jnp.int32, (m, n), 1)
    causal_at = [col < row + off for off in range(0, n, m)]

    tiles = []
    for qi in range(seq // m):
        last = qi * m // n
        tiles += [(qi, c, causal_at[qi * m % n // m] if c == last else None, c == last, c == 0)
                  for c in range(last, -1, -1)]

    scale2, bias2 = scale * LOG2E, bias * LOG2E

    def logits(qi, c, diag, first, final):
        return _mm_nt(q_ref[qi * m:(qi + 1) * m, :], kb_ref[c * n:(c + 1) * n, :]) * scale2 + bias2

    def keep(qi, c, diag, first, final, z):
        sp = jnp.maximum(z, 0.0) + jnp.log(1.0 + jnp.exp2(_neg_abs(z))) * LOG2E
        drop = sp if diag is None else jnp.where(diag, sp, 0.0)
        return z - sp, _suffix_sum(drop, tri2), jnp.sum(drop, axis=1, keepdims=True)

    acc = run = None

    def weigh(qi, c, diag, first, final, ls, suffix, tot):
        nonlocal run
        e = ls - suffix
        a = jnp.exp2(e if first else e - run)
        if diag is not None:
            a = jnp.where(diag, a, 0.0)
        run = tot if first else run + tot
        return _mm(a.astype(BF16), vb_ref[c * n:(c + 1) * n, :])

    def collect(qi, c, diag, first, final, term):
        nonlocal acc
        acc = term if first else acc + term
        if final:
            o_ref[qi * m:(qi + 1) * m, :] = acc.astype(BF16)

    n_tiles = len(tiles)
    zs, ks, ws = {}, {}, {}
    for i in range(n_tiles + 3):
        if i < n_tiles:
            zs[i] = logits(*tiles[i])
        if 1 <= i <= n_tiles:
            ks[i - 1] = keep(*tiles[i - 1], zs.pop(i - 1))
        if 2 <= i <= n_tiles + 1:
            ws[i - 2] = weigh(*tiles[i - 2], *ks.pop(i - 2))
        if i >= 3:
            collect(*tiles[i - 3], ws.pop(i - 3))


def _sb_prompt(sbq, sbk, sbv, bias, batch, seq, layer):
    spec = pl.BlockSpec((seq, SB_DH), lambda b, h: (b, h))
    kv_spec = pl.BlockSpec((1, seq * SB_HEADS, SB_DH), lambda b, h: (layer, b, 0))
    return pl.pallas_call(
        _sb_prompt_kernel,
        grid=(batch, SB_HEADS),
        in_specs=[pl.BlockSpec(memory_space=pltpu.SMEM), spec, kv_spec, kv_spec],
        out_specs=spec,
        out_shape=jax.ShapeDtypeStruct((batch * seq, SB_W), BF16),
        scratch_shapes=[pltpu.VMEM((seq, SB_DH), BF16), pltpu.VMEM((seq, SB_DH), BF16)],
        compiler_params=_cparams("parallel", "parallel"),
        name="sb_prompt",
    )(bias, sbq, sbk, sbv)


def _sb_sample_kernel(pt_ref, bias_ref, q_ref, kn_ref, vn_ref, k_hbm, v_hbm, o_ref, kbuf, vbuf, newk, newv, sem,
                      *, layer, n_pages, steps):
    b = pl.program_id(0)
    slot = b % 2
    H = SB_HEADS
    hbits = H.bit_length() - 1
    nq = steps * H
    page_rows = k_hbm.shape[2]
    n = SB_TILE
    n_chunks = n_pages * page_rows // (H * n)
    scale = SB_DH ** -0.5

    def copies(bb, sl):
        out = []
        for p in range(n_pages):
            pg = pt_ref[bb * n_pages + p]
            rows = pl.ds(p * page_rows, page_rows)
            out.append(pltpu.make_async_copy(k_hbm.at[layer, pg], kbuf.at[sl, rows, :], sem.at[0, sl]))
            out.append(pltpu.make_async_copy(v_hbm.at[layer, pg], vbuf.at[sl, rows, :], sem.at[1, sl]))
        return out

    def tile(buf, g):
        heads = [buf[slot, pl.ds(g * n * H + h, n, stride=H), :] for h in range(H)]
        return jnp.concatenate(heads, axis=1).astype(BF16)

    @pl.when(b == 0)
    def _():
        for c in copies(0, 0):
            c.start()

    @pl.when(b + 1 < pl.num_programs(0))
    def _():
        for c in copies(b + 1, 1 - slot):
            c.start()

    rw = lax.broadcasted_iota(jnp.int32, (nq, SB_W), 0)
    cw = lax.broadcasted_iota(jnp.int32, (nq, SB_W), 1)
    own = (rw & (H - 1)) == (cw >> (SB_DH.bit_length() - 1))
    q = q_ref[0]
    wq = jnp.where(own, jnp.concatenate([q] * H, axis=1), jnp.zeros((nq, SB_W), BF16))
    rn = lax.broadcasted_iota(jnp.int32, (nq, n), 0)
    cn = lax.broadcasted_iota(jnp.int32, (nq, n), 1)
    rh = rn & (H - 1)
    bias = jnp.where(rh == 0, bias_ref[0], jnp.where(rh == 1, bias_ref[1],
                     jnp.where(rh == 2, bias_ref[2], bias_ref[3])))
    tri = _strict_lower(n)

    @pl.when(b == 0)
    def _():
        newk[...] = jnp.zeros(newk.shape, F32)
        newv[...] = jnp.zeros(newv.shape, F32)

    newk[0:steps, :] = kn_ref[0]
    newv[0:steps, :] = vn_ref[0]
    valid = cn < (rn >> hbits)

    def scores(k_rows):
        z = _mm_nt(wq, k_rows) * scale + bias
        return z, _softplus(z)

    z, sp = scores(newk[...].astype(BF16))
    parts = [(z, sp, jnp.where(valid, -sp, 0.0), None)]

    for c in copies(b, slot):
        c.wait()

    for g in range(n_chunks - 1, -1, -1):
        z, sp = scores(tile(kbuf, g))
        parts.append((z, sp, -sp, g))

    split = []
    for _, _, lk, _ in parts:
        hi = lk.astype(BF16)
        split += [hi, (lk - hi.astype(F32)).astype(BF16)]
    suffix = _mm(jnp.concatenate(split, axis=0), tri)

    res = run = None
    for i, (z, sp, lk, g) in enumerate(parts):
        e = z - sp + suffix[2 * i * nq:(2 * i + 1) * nq] + suffix[(2 * i + 1) * nq:(2 * i + 2) * nq]
        if run is not None:
            e = e + run
        a = jnp.exp(e)
        if g is None:
            a = jnp.where(valid, a, 0.0)
            v_rows = newv[...].astype(BF16)
        else:
            v_rows = tile(vbuf, g)
        term = _mm(a.astype(BF16), v_rows)
        tot = jnp.sum(lk, axis=1, keepdims=True)
        res = term if res is None else res + term
        run = tot if run is None else run + tot

    out = jnp.where(own, res, 0.0)
    acc = out[:, 0:SB_DH]
    for h in range(1, H):
        acc = acc + out[:, h * SB_DH:(h + 1) * SB_DH]
    o_ref[0] = acc.astype(BF16)


def _sb_sample(page_table, bias, q16, k_new, v_new, cache_k, cache_v, layer, steps):
    batch, n_pages = page_table.shape
    nq = steps * SB_HEADS
    depth, n_phys, page_size = cache_k.shape[:3]
    cache_k = cache_k.reshape(depth, n_phys, page_size * SB_HEADS, SB_DH)
    cache_v = cache_v.reshape(depth, n_phys, page_size * SB_HEADS, SB_DH)
    past_rows = n_pages * page_size * SB_HEADS
    row = pl.BlockSpec((1, nq, SB_DH), lambda b, pt: (b, 0, 0))
    new = pl.BlockSpec((1, steps, SB_W), lambda b, pt: (b, 0, 0))
    return pl.pallas_call(
        functools.partial(_sb_sample_kernel, layer=layer, n_pages=n_pages, steps=steps),
        grid_spec=pltpu.PrefetchScalarGridSpec(
            num_scalar_prefetch=1,
            grid=(batch,),
            in_specs=[
                pl.BlockSpec(memory_space=pltpu.SMEM),
                row, new, new,
                pl.BlockSpec(memory_space=pl.ANY),
                pl.BlockSpec(memory_space=pl.ANY),
            ],
            out_specs=row,
            scratch_shapes=[
                pltpu.VMEM((2, past_rows, SB_DH), F32),
                pltpu.VMEM((2, past_rows, SB_DH), F32),
                pltpu.VMEM((SB_TILE, SB_W), F32),
                pltpu.VMEM((SB_TILE, SB_W), F32),
                pltpu.SemaphoreType.DMA((2, 2)),
            ],
        ),
        out_shape=jax.ShapeDtypeStruct((batch, nq, SB_DH), BF16),
        compiler_params=_cparams("arbitrary"),
        name="sb_sample",
    )(page_table.reshape(-1), bias, q16, k_new, v_new, cache_k, cache_v)


def _lru_kernel(x_ref, hist_ref, h0_ref, cw_ref, cb_ref, wa_ref, ba_ref, wx_ref, bx_ref, lam_ref,
                h_ref, hl_ref, tail, hc, a_s, b_s, *xt, sub):
    token_major = bool(xt)
    taps = LRU_CONV
    if token_major:
        rows, tc, width = x_ref.shape
        xt[0][...] = jnp.swapaxes(x_ref[...], 0, 1)
        x_ref = xt[0]
    else:
        tc, rows, width = x_ref.shape

    @pl.when(pl.program_id(0) == 0)
    def _():
        tail[...] = hist_ref[...]
        hc[...] = h0_ref[...]

    lam = lam_ref[...]
    log_lam = -(jnp.maximum(-lam, 0.0) + jnp.log1p(jnp.exp(-jnp.abs(lam))))
    for j in range(tc // sub):
        t0 = j * sub
        if t0 < taps - 1:
            xe = jnp.concatenate([tail[t0:taps - 1], x_ref[0:t0 + sub]], axis=0)
        else:
            xe = x_ref[t0 - (taps - 1):t0 + sub]
        xc = cb_ref[...] + xe[0:sub] * cw_ref[0]
        for i in range(1, taps):
            xc = xc + xe[i:i + sub] * cw_ref[i]
        xc = xc.reshape(sub * rows, width)
        r_parts, i_parts = [], []
        for n in range(LRU_BLOCKS):
            blk = xc[:, n * LRU_BW:(n + 1) * LRU_BW].astype(BF16)
            r_parts.append(_mm(blk, wa_ref[n]))
            i_parts.append(_mm(blk, wx_ref[n]))
        r = 0.5 * jnp.tanh(0.5 * (jnp.concatenate(r_parts, axis=1) + ba_ref[...])) + 0.5
        gate_i = 0.5 * jnp.tanh(0.5 * (jnp.concatenate(i_parts, axis=1) + bx_ref[...])) + 0.5
        log_a = LRU_C * r * log_lam
        a = jnp.exp(log_a)
        a_s[t0:t0 + sub] = a.reshape(sub, rows, width)
        one_minus_a2 = -jnp.tanh(log_a) * (a * a + 1.0)
        b_s[t0:t0 + sub] = (jnp.sqrt(one_minus_a2) * (gate_i * xc)).reshape(sub, rows, width)
    tail[...] = x_ref[tc - (taps - 1):tc]

    hist_dst = a_s if token_major else h_ref

    def step(t, h):
        h = a_s[t] * h + b_s[t]
        hist_dst[t] = h
        return h

    h = lax.fori_loop(0, tc, step, hc[...], unroll=min(tc, 8))
    hc[...] = h
    hl_ref[...] = h
    if token_major:
        h_ref[...] = jnp.swapaxes(a_s[...], 0, 1).astype(h_ref.dtype)


def _lru(x, hist, h0, cw, cb, wa, ba, wx, bx, lam, tc, token_major):
    if token_major:
        rows, steps, width = x.shape
        blk = pl.BlockSpec((rows, tc, width), lambda i: (0, i, 0))
        h_shape = jax.ShapeDtypeStruct((rows, steps, width), BF16)
    else:
        steps, rows, width = x.shape
        blk = pl.BlockSpec((tc, rows, width), lambda i: (i, 0, 0))
        h_shape = jax.ShapeDtypeStruct((steps, rows, width), F32)
    sub = max(1, LRU_ROW_CHUNK // rows)
    vec = _resident((1, width))
    time_major_buf = pltpu.VMEM((tc, rows, width), F32)
    return pl.pallas_call(
        functools.partial(_lru_kernel, sub=sub),
        grid=(steps // tc,),
        in_specs=[
            blk, _resident((LRU_CONV - 1, rows, width)), _resident((rows, width)),
            _resident((LRU_CONV, 1, width)), vec,
            _resident((LRU_BLOCKS, LRU_BW, LRU_BW)), vec,
            _resident((LRU_BLOCKS, LRU_BW, LRU_BW)), vec, vec,
        ],
        out_specs=[blk, pl.BlockSpec((rows, width), lambda i: (0, 0))],
        out_shape=[h_shape, jax.ShapeDtypeStruct((rows, width), F32)],
        scratch_shapes=[
            pltpu.VMEM((LRU_CONV - 1, rows, width), F32),
            pltpu.VMEM((rows, width), F32),
            time_major_buf, time_major_buf,
        ] + ([time_major_buf] if token_major else []),
        compiler_params=_cparams("arbitrary"),
        name="lru",
    )(x, hist, h0, cw, cb, wa, ba, wx, bx, lam)


def _merge_kernel(x_ref, oret_ref, osb_ref, olru_ref, g_ref, wgl_ref, wr_ref, ws_ref, wl_ref, wo_ref,
                  out_ref, m_ref):
    x = x_ref[...]
    xn = _rms(x, g_ref[...]).astype(BF16)
    branches = ((oret_ref, wr_ref), (osb_ref, ws_ref), (olru_ref, wl_ref))
    cw = 256
    for c in range(D_MODEL // cw):
        cols = slice(c * cw, (c + 1) * cw)
        m = None
        for j, (o_ref, w_ref) in enumerate(branches):
            lo = PROJ_COLS + j * D_MODEL + c * cw
            gate = jax.nn.sigmoid(_mm(xn, wgl_ref[0, :, lo:lo + cw]))
            term = gate * _mm(o_ref[...], w_ref[0, :, cols])
            m = term if m is None else m + term
        m_ref[:, cols] = m.astype(BF16)
    out_ref[...] = x + _mm(m_ref[...], wo_ref[0])


def _merge(x, o_ret, o_sb, o_lru, g, w_in, wr, ws, wl, wo, layer):
    n = x.shape[0]
    tm = TOKEN_TILE
    row = lambda width: pl.BlockSpec((tm, width), lambda i: (i, 0))
    return pl.pallas_call(
        _merge_kernel,
        grid=(n // tm,),
        in_specs=[
            row(D_MODEL), row(RET_W), row(SB_W), row(LRU_WIDTH),
            _resident((1, D_MODEL)), _layer_block(w_in, layer),
            _layer_block(wr, layer), _layer_block(ws, layer), _layer_block(wl, layer),
            _layer_block(wo, layer),
        ],
        out_specs=row(D_MODEL),
        out_shape=jax.ShapeDtypeStruct((n, D_MODEL), F32),
        scratch_shapes=[pltpu.VMEM((tm, D_MODEL), BF16)],
        compiler_params=_cparams("parallel"),
        name="merge",
    )(x, o_ret, o_sb, o_lru, g, w_in, wr, ws, wl, wo)


def _ffn_kernel(x_ref, g_ref, wg_ref, wu_ref, cw_ref, cb_ref, wd_ref, hist_ref, nf_ref, out_ref, st_ref,
                carry, h_ref, *, shift, tiles_per_seq, final_norm):
    tm = x_ref.shape[0]
    hist_rows = carry.shape[0]
    i = pl.program_id(0)

    @pl.when(i % tiles_per_seq == 0)
    def _():
        carry[...] = hist_ref[...]

    x = x_ref[...]
    xn = _rms(x, g_ref[...]).astype(BF16)
    S = V7X_SUBLANES
    n_chunks = D_FF // FFN_COL_CHUNK
    col_slice = lambda c: slice(c * FFN_COL_CHUNK, (c + 1) * FFN_COL_CHUNK)

    def up(c):
        return _mm(xn, wg_ref[0, :, col_slice(c)]), _mm(xn, wu_ref[0, :, col_slice(c)])

    def act(c, g, u):
        cols = col_slice(c)
        hist = carry[:, cols]
        if shift == 1:
            ext = jnp.concatenate([hist, g[0:S]], axis=0)
            p1 = jnp.concatenate([pltpu.roll(ext, 1, 0)[S:2 * S], pltpu.roll(g, 1, 0)[S:]], axis=0)
            p2 = jnp.concatenate([pltpu.roll(ext, 2, 0)[S:2 * S], pltpu.roll(g, 2, 0)[S:]], axis=0)
        else:
            p1 = jnp.concatenate([hist[shift:2 * shift], g[0:tm - shift]], axis=0)
            p2 = jnp.concatenate([hist[0:2 * shift], g[0:tm - 2 * shift]], axis=0)
        carry[:, cols] = g[tm - hist_rows:tm]
        gc = cb_ref[:, cols] + p2 * cw_ref[0, :, cols] + p1 * cw_ref[1, :, cols] + g * cw_ref[2, :, cols]
        return (jax.nn.gelu(gc) * u).astype(BF16)

    ups = {}
    for c in range(n_chunks + 1):
        if c < n_chunks:
            ups[c] = up(c)
        if c >= 1:
            h_ref[:, col_slice(c - 1)] = act(c - 1, *ups.pop(c - 1))
    st_ref[...] = carry[hist_rows - 2 * shift:hist_rows, :].reshape(st_ref.shape)
    y = x + _mm(h_ref[...], wd_ref[0])
    if final_norm:
        y = _rms(y, nf_ref[...])
    out_ref[...] = y


def _ffn(x, g, wg, wu, cw, cb, wd, hist, nf, layer, shift, tiles_per_seq, final_norm):
    n = x.shape[0]
    tm = TOKEN_TILE
    n_seq = n // (tm * tiles_per_seq)
    hist_rows = hist.shape[0]
    row = pl.BlockSpec((tm, D_MODEL), lambda i: (i, 0))
    return pl.pallas_call(
        functools.partial(_ffn_kernel, shift=shift, tiles_per_seq=tiles_per_seq, final_norm=final_norm),
        grid=(n // tm,),
        in_specs=[
            row, _resident((1, D_MODEL)),
            _layer_block(wg, layer), _layer_block(wu, layer),
            _resident((FFN_CONV, 1, D_FF)), _resident((1, D_FF)),
            _layer_block(wd, layer),
            _resident((hist_rows, D_FF)), _resident((1, D_MODEL)),
        ],
        out_specs=[row, pl.BlockSpec((1, 2 * shift, D_FF), lambda i: (i // tiles_per_seq, 0, 0))],
        out_shape=[
            jax.ShapeDtypeStruct((n, D_MODEL), F32),
            jax.ShapeDtypeStruct((n_seq, 2 * shift, D_FF), F32),
        ],
        scratch_shapes=[pltpu.VMEM((hist_rows, D_FF), F32), pltpu.VMEM((tm, D_FF), BF16)],
        compiler_params=_cparams("arbitrary"),
        name="ffn",
    )(x, g, wg, wu, cw, cb, wd, hist, nf)


def _rot_tables(pos):
    half = RET_DK // 2
    inv = ROPE_BASE ** (-jnp.arange(half, dtype=F32) / half)
    ang = pos[:, None] * inv[None, :]
    c, s = jnp.cos(ang), jnp.sin(ang)
    return jnp.concatenate([c, c], axis=-1), jnp.concatenate([-s, s], axis=-1)


def _layer_weights(l, p):
    return dict(
        layer=l,
        norm1=p["norm1"][l][None, :],
        w_in=p["w_in"],
        ret_gn=p["ret_gn"][l][None, :],
        sb_bias=p["sb_bias"][l],
        lru_cw=p["lru_conv_w"][l][:, None, :],
        lru_cb=p["lru_conv_b"][l][None, :],
        lru_wa=p["lru_w_a"][l].astype(BF16),
        lru_ba=p["lru_b_a"][l][None, :],
        lru_wx=p["lru_w_x"][l].astype(BF16),
        lru_bx=p["lru_b_x"][l][None, :],
        lru_lam=p["lru_lambda"][l][None, :],
        w_br_ret=p["w_br_ret"], w_br_sb=p["w_br_sb"], w_br_lru=p["w_br_lru"], w_out=p["w_out"],
        norm2=p["norm2"][l][None, :],
        w_ffn_gate=p["w_ffn_gate"], w_ffn_up=p["w_ffn_up"],
        ffn_cw=p["ffn_conv_w"][l][:, None, :],
        ffn_cb=p["ffn_conv_b"][l][None, :],
        w_ffn_down=p["w_ffn_down"],
        norm_f=p["norm_f"][None, :],
    )


def _mix_and_ffn(x, o_ret, o_sb, o_lru, lw, ffn_hist, shift, tiles_per_seq, final_norm):
    x1 = _merge(x, o_ret, o_sb, o_lru, lw["norm1"], lw["w_in"], lw["w_br_ret"], lw["w_br_sb"],
                lw["w_br_lru"], lw["w_out"], lw["layer"])
    return _ffn(x1, lw["norm2"], lw["w_ffn_gate"], lw["w_ffn_up"], lw["ffn_cw"], lw["ffn_cb"],
                lw["w_ffn_down"], ffn_hist, lw["norm_f"], lw["layer"], shift, tiles_per_seq, final_norm)


def _prompt_layer(x, lw, lg, cos, sin, batch, seq, layer, depth, kv_stack, final_norm):
    retq, ret, sbq, sbk, sbv, lx = _proj(x, lw["norm1"], lw["w_in"], layer, cos, sin,
                                         stacked=(depth,) + kv_stack)
    o_ret, s_new = _ret_prompt(retq, ret, lg, lw["ret_gn"], batch, seq)
    o_sb = _sb_prompt(sbq, sbk, sbv, lw["sb_bias"], batch, seq, layer)
    lx3 = lx.reshape(batch, seq, LRU_WIDTH)
    h, h_last = _lru(
        lx3, jnp.zeros((LRU_CONV - 1, batch, LRU_WIDTH), F32), jnp.zeros((batch, LRU_WIDTH), F32),
        lw["lru_cw"], lw["lru_cb"], lw["lru_wa"], lw["lru_ba"], lw["lru_wx"], lw["lru_bx"], lw["lru_lam"],
        tc=256, token_major=True)
    o_lru = h.reshape(batch * seq, LRU_WIDTH)
    x2, ffn_state = _mix_and_ffn(x, o_ret, o_sb, o_lru, lw, jnp.zeros((V7X_SUBLANES, D_FF), F32),
                                 shift=1, tiles_per_seq=seq // TOKEN_TILE, final_norm=final_norm)
    states = (s_new, h_last, lx3[:, seq - (LRU_CONV - 1):, :], ffn_state)
    return x2, (sbk, sbv), states


def _sample_layer(x, lw, lg, cos, sin, batch, steps, layer, page_table, cache_k, cache_v,
                  state_ret, ret_out, state_lru_h, state_lru_conv, state_ffn_conv, final_norm):
    to_tok = lambda a: a.reshape(steps, batch, -1).transpose(1, 0, 2).reshape(batch * steps, -1)
    to_tm = lambda a: a.reshape(batch, steps, -1).transpose(1, 0, 2).reshape(steps * batch, -1)
    heads = lambda a: a.reshape(steps, batch, SB_HEADS, SB_DH).transpose(1, 0, 2, 3)
    retq, ret, sbq, sbk, sbv, lx = _proj(x, lw["norm1"], lw["w_in"], layer, cos, sin)
    o_ret_tok, ret_out = _ret_sample(to_tok(retq), to_tok(ret), lg, lw["ret_gn"], state_ret, steps, layer,
                                     ret_out)
    o_ret = to_tm(o_ret_tok)
    nq = steps * SB_HEADS
    k_new, v_new = heads(sbk), heads(sbv)
    o16 = _sb_sample(page_table, lw["sb_bias"], heads(sbq).reshape(batch, nq, SB_DH),
                     k_new.reshape(batch, steps, SB_W), v_new.reshape(batch, steps, SB_W),
                     cache_k, cache_v, layer, steps)
    o_sb = to_tm(o16.reshape(batch * steps, SB_W))
    lx3 = lx.reshape(steps, batch, LRU_WIDTH)
    h_tm, h_last = _lru(
        lx3, state_lru_conv.transpose(1, 0, 2), state_lru_h,
        lw["lru_cw"], lw["lru_cb"], lw["lru_wa"], lw["lru_ba"], lw["lru_wx"], lw["lru_bx"], lw["lru_lam"],
        tc=steps, token_major=False)
    o_lru = h_tm.reshape(steps * batch, LRU_WIDTH).astype(BF16)
    ffn_hist = state_ffn_conv.transpose(1, 0, 2).reshape((FFN_CONV - 1) * batch, D_FF)
    x2, ffn_state = _mix_and_ffn(x, o_ret, o_sb, o_lru, lw, ffn_hist, shift=batch, tiles_per_seq=1,
                                 final_norm=final_norm)
    states = (
        k_new, v_new, h_last,
        lx3[steps - (LRU_CONV - 1):].transpose(1, 0, 2),
        ffn_state.reshape(FFN_CONV - 1, batch, D_FF).transpose(1, 0, 2),
    )
    return x2, ret_out, states


def kernel(x_prompt, x_sample, cache_sb_k, cache_sb_v, state_ret, state_lru_h, state_lru_conv, state_ffn_conv,
           page_table, norm1, w_in, ret_gn, sb_bias, lru_conv_w, lru_conv_b, lru_w_a, lru_b_a, lru_w_x, lru_b_x,
           lru_lambda, w_br_ret, w_br_sb, w_br_lru, w_out, norm2, w_ffn_gate, w_ffn_up, ffn_conv_w, ffn_conv_b,
           w_ffn_down, norm_f):
    bf = lambda w: w.astype(BF16)
    params = dict(norm1=norm1, w_in=bf(w_in), ret_gn=ret_gn, sb_bias=sb_bias, lru_conv_w=lru_conv_w,
                  lru_conv_b=lru_conv_b, lru_w_a=lru_w_a, lru_b_a=lru_b_a, lru_w_x=lru_w_x, lru_b_x=lru_b_x,
                  lru_lambda=lru_lambda, w_br_ret=bf(w_br_ret), w_br_sb=bf(w_br_sb), w_br_lru=bf(w_br_lru),
                  w_out=bf(w_out), norm2=norm2, w_ffn_gate=bf(w_ffn_gate), w_ffn_up=bf(w_ffn_up),
                  ffn_conv_w=ffn_conv_w, ffn_conv_b=ffn_conv_b, w_ffn_down=bf(w_ffn_down), norm_f=norm_f)
    depth = w_in.shape[0]
    batch, seq, _ = x_prompt.shape
    dec_batch, steps, _ = x_sample.shape
    past_len = page_table.shape[1] * cache_sb_k.shape[2]
    assert seq % TOKEN_TILE == 0 and dec_batch * steps == TOKEN_TILE and steps & (steps - 1) == 0
    assert steps >= LRU_CONV - 1 and steps >= FFN_CONV - 1

    lg = jnp.log(1.0 - 2.0 ** (-5.0 - jnp.arange(RET_HEADS, dtype=F32)))
    cos_p, sin_p = _rot_tables(jnp.arange(seq, dtype=F32))
    cos_s, sin_s = _rot_tables(past_len + jnp.repeat(jnp.arange(steps, dtype=F32), dec_batch))

    xp = x_prompt.reshape(batch * seq, D_MODEL)
    xs = x_sample.transpose(1, 0, 2).reshape(steps * dec_batch, D_MODEL)
    st_p, st_s = [], []
    kv_stack = (None, None)
    ret_sample = None
    for l in range(depth):
        lw = _layer_weights(l, params)
        last = l == depth - 1
        xp, kv_stack, st = _prompt_layer(xp, lw, lg, cos_p, sin_p, batch, seq, l, depth, kv_stack, last)
        st_p.append(st)
        xs, ret_sample, st = _sample_layer(xs, lw, lg, cos_s, sin_s, dec_batch, steps, l, page_table, cache_sb_k,
                                           cache_sb_v, state_ret, ret_sample, state_lru_h[l], state_lru_conv[l],
                                           state_ffn_conv[l], last)
        st_s.append(st)
    y_prompt = xp.reshape(batch, seq, D_MODEL)
    y_sample = xs.reshape(steps, dec_batch, D_MODEL).transpose(1, 0, 2)
    k_prompt, v_prompt = (a.reshape(depth, batch, seq, SB_HEADS, SB_DH) for a in kv_stack)
    stack = lambda sts, j: jnp.stack([s[j] for s in sts], axis=0)
    return (y_prompt, y_sample,
            k_prompt, v_prompt, stack(st_s, 0), stack(st_s, 1),
            stack(st_p, 0), ret_sample,
            stack(st_p, 1), stack(st_s, 2),
            stack(st_p, 2), stack(st_s, 3),
            stack(st_p, 3), stack(st_s, 4))
```

```python
import functools

import jax
import jax.numpy as jnp
from jax import lax
from jax.experimental import pallas as pl
from jax.experimental.pallas import tpu as pltpu

F32 = jnp.float32
BF16 = jnp.bfloat16

D_MODEL = 1024
RET_HEADS = 4
RET_DK = 128
RET_DV = 128
RET_CHUNK = 128
ROPE_BASE = 10000.0
SB_HEADS = 4
SB_DH = 128
SB_TILE = 256
SB_Q_ROWS = 256
LRU_WIDTH = 512
LRU_BLOCKS = 4
LRU_BW = LRU_WIDTH // LRU_BLOCKS
LRU_CONV = 4
LRU_C = 8.0
N_BRANCH = 3
D_FF = 2816
FFN_CONV = 3
EPS = 1e-6

RET_W = RET_HEADS * RET_DK
SB_W = SB_HEADS * SB_DH
PROJ_COLS = 4 * RET_W + 3 * SB_W + LRU_WIDTH

V7X_LANES = 128
V7X_SUBLANES = 8
V7X_VMEM_LIMIT_BYTES = 60000 * 1024

TOKEN_TILE = 512
FFN_COL_CHUNK = 256
LRU_ROW_CHUNK = 256


def _cparams(*sem):
    return pltpu.CompilerParams(dimension_semantics=sem, vmem_limit_bytes=V7X_VMEM_LIMIT_BYTES)


def _resident(shape):
    zeros = (0,) * len(shape)
    return pl.BlockSpec(shape, lambda *_: zeros, pipeline_mode=pl.Buffered(1))


def _layer_block(w, layer):
    zeros = (0,) * (w.ndim - 1)
    return pl.BlockSpec((1,) + w.shape[1:], lambda *_: (layer,) + zeros, pipeline_mode=pl.Buffered(1))


def _mm(a, b):
    return jnp.dot(a, b, preferred_element_type=F32)


def _mm_nt(a, b):
    return lax.dot_general(a, b, (((1,), (1,)), ((), ())), preferred_element_type=F32)


def _mm_tn(a, b):
    return lax.dot_general(a, b, (((0,), (0,)), ((), ())), preferred_element_type=F32)


def _rms(x, g):
    return x * lax.rsqrt(jnp.mean(x * x, axis=-1, keepdims=True) + EPS) * g


LOG2E = 1.4426950408889634


def _neg_abs(x):
    bits = lax.bitcast_convert_type(x, jnp.uint32) | jnp.uint32(0x80000000)
    return lax.bitcast_convert_type(bits, F32)


def _softplus(z):
    return jnp.maximum(z, 0.0) + jnp.log(1.0 + jnp.exp(-jnp.abs(z)))


def _suffix_sum(lk, tri):
    hi = lk.astype(BF16)
    lo = (lk - hi.astype(F32)).astype(BF16)
    if tri.shape[0] == 2 * lk.shape[1]:
        return _mm(jnp.concatenate([hi, lo], axis=1), tri)
    return _mm(hi, tri) + _mm(lo, tri)


def _strict_lower(n):
    r = lax.broadcasted_iota(jnp.int32, (n, n), 0)
    c = lax.broadcasted_iota(jnp.int32, (n, n), 1)
    return (r > c).astype(BF16)


def _proj_kernel(x_ref, g_ref, w_ref, cos_ref, sin_ref, retq_ref, ret_ref, sbq_ref, sbk_ref, sbv_ref, lx_ref):
    w = lambda lo, width: w_ref[0, :, lo:lo + width]
    xn = _rms(x_ref[...], g_ref[...]).astype(BF16)
    cs, sn = cos_ref[...], sin_ref[...]
    rq = _mm(xn, w(0, RET_W))
    rk = _mm(xn, w(RET_W, RET_W))
    for h in range(RET_HEADS):
        sl = slice(h * RET_DK, (h + 1) * RET_DK)
        retq_ref[:, sl] = _rotary(rq[:, sl], cs, sn).astype(BF16)
        ret_ref[:, sl] = _rotary(rk[:, sl], cs, sn) * (RET_DK ** -0.5)
    for c in range(2, 4):
        ret_ref[:, (c - 1) * RET_W:c * RET_W] = _mm(xn, w(c * RET_W, RET_W))
    o = 4 * RET_W
    sbq_ref[...] = _mm(xn, w(o, SB_W)).astype(BF16)
    lx_ref[...] = _mm(xn, w(o + 3 * SB_W, LRU_WIDTH))
    for ref, lo in ((sbk_ref, o + SB_W), (sbv_ref, o + 2 * SB_W)):
        kv = _mm(xn, w(lo, SB_W))
        if len(ref.shape) == 2:
            ref[...] = kv
        else:
            tm = kv.shape[0]
            for h in range(SB_HEADS):
                ref[0, pl.ds(h, tm, stride=SB_HEADS), :] = kv[:, h * SB_DH:(h + 1) * SB_DH]
            for later in range(1, ref.shape[0]):
                ref[later] = jnp.zeros(ref.shape[1:], F32)


def _proj_kernel_stacked(x_ref, g_ref, w_ref, cos_ref, sin_ref, prev_k, prev_v, *out_refs):
    _proj_kernel(x_ref, g_ref, w_ref, cos_ref, sin_ref, *out_refs)


def _proj(x, g, w_in, layer, cos, sin, stacked=None):
    n = x.shape[0]
    tm = TOKEN_TILE
    row = lambda width: pl.BlockSpec((tm, width), lambda i: (i, 0))
    period = cos.shape[0] // tm
    table = pl.BlockSpec((tm, RET_DK), lambda i: (i % period, 0))
    in_specs = [row(D_MODEL), _resident((1, D_MODEL)), _layer_block(w_in, layer), table, table]
    args = [x, g, w_in, cos, sin]
    body, aliases = _proj_kernel, {}
    if stacked is None:
        kv_spec, kv_shape = row(SB_W), jax.ShapeDtypeStruct((n, SB_W), F32)
    else:
        depth, prev_k, prev_v = stacked
        kv_shape = jax.ShapeDtypeStruct((depth, n * SB_HEADS, SB_DH), F32)
        if prev_k is None:
            assert layer == 0
            kv_spec = pl.BlockSpec((depth, tm * SB_HEADS, SB_DH), lambda i: (0, i, 0))
        else:
            kv_spec = pl.BlockSpec((1, tm * SB_HEADS, SB_DH), lambda i: (layer, i, 0))
            body, aliases = _proj_kernel_stacked, {5: 3, 6: 4}
            in_specs += [pl.BlockSpec(memory_space=pl.ANY)] * 2
            args += [prev_k, prev_v]
    return pl.pallas_call(
        body,
        grid=(n // tm,),
        in_specs=in_specs,
        out_specs=[row(RET_W), row(3 * RET_W), row(SB_W), kv_spec, kv_spec, row(LRU_WIDTH)],
        out_shape=[
            jax.ShapeDtypeStruct((n, RET_W), BF16),
            jax.ShapeDtypeStruct((n, 3 * RET_W), F32),
            jax.ShapeDtypeStruct((n, SB_W), BF16),
            kv_shape, kv_shape,
            jax.ShapeDtypeStruct((n, LRU_WIDTH), F32),
        ],
        input_output_aliases=aliases,
        compiler_params=_cparams("parallel"),
        name="proj",
    )(*args)


def _rotary(x, cs, sn):
    return x * cs + pltpu.roll(x, RET_DK // 2, 1) * sn


def _group_norm_gate(o, gn, gate):
    mu = jnp.mean(o, axis=-1, keepdims=True)
    d = o - mu
    var = jnp.mean(d * d, axis=-1, keepdims=True)
    return d * lax.rsqrt(var + EPS) * gn * (gate * jax.nn.sigmoid(gate))


def _ret_prompt_kernel(lg_ref, q_ref, k_ref, v_ref, g_ref, gn_ref, o_ref, s_ref):
    lg = lg_ref[pl.program_id(1)]
    L = RET_CHUNK
    n_chunks = q_ref.shape[0] // L
    row = lax.broadcasted_iota(jnp.int32, (L, L), 0).astype(F32)
    col = lax.broadcasted_iota(jnp.int32, (L, L), 1).astype(F32)
    diff = row - col
    dmat = jnp.where(diff >= 0, jnp.exp(jnp.maximum(diff, 0.0) * lg), 0.0)
    q_dec = jnp.exp((row + 1.0) * lg)
    k_dec = jnp.exp((L - 1.0 - row) * lg)
    s_dec = jnp.exp(jnp.full((RET_DK, RET_DV), float(L), F32) * lg)
    gn = gn_ref[...]

    rows_of = lambda c: slice(c * L, (c + 1) * L)

    def local(c):
        rows = rows_of(c)
        qb, k = q_ref[rows, :], k_ref[rows, :]
        vb = v_ref[rows, :].astype(BF16)
        scores = (_mm_nt(qb, k.astype(BF16)) * dmat).astype(BF16)
        return qb, vb, scores, _mm_tn((k * k_dec).astype(BF16), vb)

    s = jnp.zeros((RET_DK, RET_DV), F32)

    def attend(qb, vb, scores, upd):
        nonlocal s
        o = _mm(scores, vb) + _mm(qb, s.astype(BF16)) * q_dec
        s = s_dec * s + upd
        return o

    def emit(c, o):
        rows = rows_of(c)
        o_ref[rows, :] = _group_norm_gate(o, gn, g_ref[rows, :]).astype(BF16)

    locs, outs = {}, {}
    for c in range(n_chunks + 2):
        if c < n_chunks:
            locs[c] = local(c)
        if 1 <= c <= n_chunks:
            outs[c - 1] = attend(*locs.pop(c - 1))
        if c >= 2:
            emit(c - 2, outs.pop(c - 2))
    s_ref[0, 0] = s


def _ret_prompt(retq, ret, lg, gn, batch, seq):
    H = RET_HEADS
    col = lambda j: pl.BlockSpec((seq, RET_DK), lambda b, h, j=j: (b, j * H + h))
    return pl.pallas_call(
        _ret_prompt_kernel,
        grid=(batch, H),
        in_specs=[
            pl.BlockSpec(memory_space=pltpu.SMEM),
            col(0), col(0), col(1), col(2),
            pl.BlockSpec((1, RET_DV), lambda b, h: (0, h)),
        ],
        out_specs=[
            pl.BlockSpec((seq, RET_DV), lambda b, h: (b, h)),
            pl.BlockSpec((1, 1, RET_DK, RET_DV), lambda b, h: (b, h, 0, 0)),
        ],
        out_shape=[
            jax.ShapeDtypeStruct((batch * seq, RET_W), BF16),
            jax.ShapeDtypeStruct((batch, H, RET_DK, RET_DV), F32),
        ],
        compiler_params=_cparams("parallel", "parallel"),
        name="ret_prompt",
    )(lg, retq, ret, ret, ret, gn)


def _ret_sample_kernel(lg_ref, q_ref, k_ref, v_ref, g_ref, gn_ref, st_ref, o_ref, so_ref, *, nb, steps):
    R = nb * steps
    P = RET_DK
    shift = steps.bit_length() - 1
    ri = lax.broadcasted_iota(jnp.int32, (R, P), 0)
    ci = lax.broadcasted_iota(jnp.int32, (R, P), 1)
    rb, rt = ri >> shift, ri & (steps - 1)
    cb, ct = ci >> shift, ci & (steps - 1)
    same = (rb == cb) & (ci < R) & (ct <= rt)
    dt = jnp.maximum(rt - ct, 0).astype(F32)
    t_row = rt.astype(F32)
    wide_r = lax.broadcasted_iota(jnp.int32, (R, nb * RET_DV), 0) >> shift
    wide_c = lax.broadcasted_iota(jnp.int32, (R, nb * RET_DV), 1) >> (RET_DV.bit_length() - 1)
    own = wide_r == wide_c
    zpad = jnp.zeros((P - R, RET_DK), BF16)
    zpad_w = jnp.zeros((P - R, nb * RET_DV), BF16)
    for h in range(RET_HEADS):
        lg = lg_ref[h]
        sl = slice(h * RET_DK, (h + 1) * RET_DK)
        dmat = jnp.where(same, jnp.exp(dt * lg), 0.0)
        q_dec = jnp.exp((t_row + 1.0) * lg)
        k_dec = jnp.exp((steps - 1.0 - t_row) * lg)
        s_dec = jnp.exp(jnp.full((RET_DK, RET_DV), float(steps), F32) * lg)
        qb, k, v = q_ref[:, sl], k_ref[:, sl], v_ref[:, sl]
        k_pad = jnp.concatenate([k.astype(BF16), zpad], axis=0)
        v_pad = jnp.concatenate([v.astype(BF16), zpad], axis=0)
        scores = _mm_nt(qb, k_pad) * dmat
        intra = _mm(scores.astype(BF16), v_pad)
        s_cat = jnp.concatenate([st_ref[0, b, h] for b in range(nb)], axis=1)
        wide = jnp.where(own, _mm(qb, s_cat.astype(BF16)), 0.0)
        cross = wide[:, 0:RET_DV]
        for b in range(1, nb):
            cross = cross + wide[:, b * RET_DV:(b + 1) * RET_DV]
        o = intra + cross * q_dec
        o_ref[:, sl] = _group_norm_gate(o, gn_ref[:, sl], g_ref[:, sl]).astype(BF16)
        kd_pad = jnp.concatenate([(k * k_dec).astype(BF16), zpad], axis=0)
        v_wide = jnp.where(own, jnp.concatenate([v] * nb, axis=1), 0.0).astype(BF16)
        upd = _mm_tn(kd_pad, jnp.concatenate([v_wide, zpad_w], axis=0))
        for b in range(nb):
            so_ref[0, b, h] = s_dec * st_ref[0, b, h] + upd[:, b * RET_DV:(b + 1) * RET_DV]
    for later in range(1, so_ref.shape[0]):
        so_ref[later] = jnp.zeros(so_ref.shape[1:], F32)


def _ret_sample_kernel_stacked(*refs, nb, steps):
    _ret_sample_kernel(*refs[:7], *refs[8:], nb=nb, steps=steps)


def _ret_sample(retq_tok, ret_tok, lg, gn, states, steps, layer, prev_out):
    n = ret_tok.shape[0]
    batch = n // steps
    nb = 8
    R = nb * steps
    col = lambda j: pl.BlockSpec((R, RET_W), lambda i, j=j: (i, j))
    st_spec = pl.BlockSpec((1, nb, RET_HEADS, RET_DK, RET_DV), lambda i: (layer, i, 0, 0, 0))
    in_specs = [
        pl.BlockSpec(memory_space=pltpu.SMEM),
        col(0), col(0), col(1), col(2),
        _resident((1, RET_W)),
        st_spec,
    ]
    args = [lg, retq_tok, ret_tok, ret_tok, ret_tok, gn, states]
    body, aliases = _ret_sample_kernel, {}
    if prev_out is None:
        assert layer == 0
        out_spec = pl.BlockSpec((states.shape[0], nb, RET_HEADS, RET_DK, RET_DV), lambda i: (0, i, 0, 0, 0))
    else:
        out_spec = st_spec
        body, aliases = _ret_sample_kernel_stacked, {7: 1}
        in_specs.append(pl.BlockSpec(memory_space=pl.ANY))
        args.append(prev_out)
    return pl.pallas_call(
        functools.partial(body, nb=nb, steps=steps),
        grid=(batch // nb,),
        in_specs=in_specs,
        out_specs=[pl.BlockSpec((R, RET_W), lambda i: (i, 0)), out_spec],
        out_shape=[
            jax.ShapeDtypeStruct((n, RET_W), BF16),
            jax.ShapeDtypeStruct(states.shape, F32),
        ],
        input_output_aliases=aliases,
        compiler_params=_cparams("parallel"),
        name="ret_sample",
    )(*args)


def _sb_prompt_body(bias_ref, q_ref, k_ref, v_ref, o_ref, kb_ref, vb_ref, interleave):
    head = pl.program_id(1)
    bias = bias_ref[head]
    n = SB_TILE
    m = SB_Q_ROWS
    seq = q_ref.shape[0]
    scale = SB_DH ** -0.5
    kb_ref[...] = k_ref[0, pl.ds(head, seq, stride=SB_HEADS), :].astype(BF16)
    vb_ref[...] = v_ref[0, pl.ds(head, seq, stride=SB_HEADS), :].astype(BF16)
    tri = _strict_lower(n)
    tri2 = jnp.concatenate([tri, tri], axis=0)
    row = lax.broadcasted_iota(jnp.int32, (m, n), 0)
    col = lax.broadcasted_iota(jnp.int32, (m, n), 1)
    causal_at = [col < row + off for off in range(0, n, m)]

    tiles = []
    for qi in range(seq // m):
        last = qi * m // n
        tiles += [(qi, c, causal_at[qi * m % n // m] if c == last else None, c == last, c == 0)
                  for c in range(last, -1, -1)]

    scale2, bias2 = scale * LOG2E, bias * LOG2E

    def logits(qi, c, diag, first, final):
        return _mm_nt(q_ref[qi * m:(qi + 1) * m, :], kb_ref[c * n:(c + 1) * n, :]) * scale2 + bias2

    def keep(qi, c, diag, first, final, z):
        sp = jnp.maximum(z, 0.0) + jnp.log(1.0 + jnp.exp2(_neg_abs(z))) * LOG2E
        drop = sp if diag is None else jnp.where(diag, sp, 0.0)
        return z - sp, _suffix_sum(drop, tri2), jnp.sum(drop, axis=1, keepdims=True)

    acc = run = None

    def weigh(qi, c, diag, first, final, ls, suffix, tot):
        nonlocal run
        e = ls - suffix
        a = jnp.exp2(e if first else e - run)
        if diag is not None:
            a = jnp.where(diag, a, 0.0)
        run = tot if first else run + tot
        return _mm(a.astype(BF16), vb_ref[c * n:(c + 1) * n, :])

    def collect(qi, c, diag, first, final, term):
        nonlocal acc
        acc = term if first else acc + term
        if final:
            o_ref[qi * m:(qi + 1) * m, :] = acc.astype(BF16)

    n_tiles = len(tiles)
    zs, ks, ws = {}, {}, {}
    for i in range(n_tiles + 3):
        if i < n_tiles:
            zs[i] = logits(*tiles[i])
        if 1 <= i <= n_tiles:
            ks[i - 1] = keep(*tiles[i - 1], zs.pop(i - 1))
        if 2 <= i <= n_tiles + 1:
            ws[i - 2] = weigh(*tiles[i - 2], *ks.pop(i - 2))
        if i >= 3:
            collect(*tiles[i - 3], ws.pop(i - 3))
        interleave(i, n_tiles + 3)


def _sb_kernel(pt_ref, bias_ref, q_ref, k_ref, v_ref, sq_ref, kn_ref, vn_ref, k_hbm, v_hbm, o_ref, so_ref,
               kb_ref, vb_ref, kbuf, vbuf, newk, newv, sem, *, layer, n_pages, steps):
    per_step = sq_ref.shape[0]
    assert per_step % 2 == 0
    step = pl.program_id(0) * pl.num_programs(1) + pl.program_id(1)
    n_steps = pl.num_programs(0) * pl.num_programs(1)
    H = SB_HEADS
    hbits = H.bit_length() - 1
    nq = steps * H
    page_rows = k_hbm.shape[2]
    n = SB_TILE
    n_chunks = n_pages * page_rows // (H * n)
    scale = SB_DH ** -0.5

    def copies(elem, slot):
        out = []
        for p in range(n_pages):
            pg = pt_ref[elem * n_pages + p]
            rows = pl.ds(p * page_rows, page_rows)
            out.append(pltpu.make_async_copy(k_hbm.at[layer, pg], kbuf.at[slot, rows, :], sem.at[0, slot]))
            out.append(pltpu.make_async_copy(v_hbm.at[layer, pg], vbuf.at[slot, rows, :], sem.at[1, slot]))
        return out

    def fetch(elem, slot):
        for c in copies(elem, slot):
            c.start()

    @pl.when(step == 0)
    def _():
        fetch(0, 0)
        fetch(1, 1)
        newk[...] = jnp.zeros(newk.shape, F32)
        newv[...] = jnp.zeros(newv.shape, F32)

    rw = lax.broadcasted_iota(jnp.int32, (nq, SB_W), 0)
    cw = lax.broadcasted_iota(jnp.int32, (nq, SB_W), 1)
    own = (rw & (H - 1)) == (cw >> (SB_DH.bit_length() - 1))
    rn = lax.broadcasted_iota(jnp.int32, (nq, n), 0)
    cn = lax.broadcasted_iota(jnp.int32, (nq, n), 1)
    rh = rn & (H - 1)
    bias = jnp.where(rh == 0, bias_ref[0], jnp.where(rh == 1, bias_ref[1],
                     jnp.where(rh == 2, bias_ref[2], bias_ref[3])))
    tri = _strict_lower(n)
    valid = cn < (rn >> hbits)

    def sample_sequence(j):
        elem = step * per_step + j
        slot = j % 2

        def tile(buf, g):
            heads = [buf[slot, pl.ds(g * n * H + h, n, stride=H), :] for h in range(H)]
            return jnp.concatenate(heads, axis=1).astype(BF16)

        wq = jnp.where(own, jnp.concatenate([sq_ref[j]] * H, axis=1), jnp.zeros((nq, SB_W), BF16))

        def scores(k_rows):
            z = _mm_nt(wq, k_rows) * scale + bias
            return z, _softplus(z)

        newk[0:steps, :] = kn_ref[j]
        newv[0:steps, :] = vn_ref[j]
        z, sp = scores(newk[...].astype(BF16))
        parts = [(z, sp, jnp.where(valid, -sp, 0.0), None)]

        for c in copies(elem, slot):
            c.wait()

        for g in range(n_chunks - 1, -1, -1):
            z, sp = scores(tile(kbuf, g))
            parts.append((z, sp, -sp, g))

        split = []
        for _, _, lk, _ in parts:
            hi = lk.astype(BF16)
            split += [hi, (lk - hi.astype(F32)).astype(BF16)]
        suffix = _mm(jnp.concatenate(split, axis=0), tri)

        res = run = None
        for i, (z, sp, lk, g) in enumerate(parts):
            e = z - sp + suffix[2 * i * nq:(2 * i + 1) * nq] + suffix[(2 * i + 1) * nq:(2 * i + 2) * nq]
            if run is not None:
                e = e + run
            a = jnp.exp(e)
            if g is None:
                a = jnp.where(valid, a, 0.0)
                v_rows = newv[...].astype(BF16)
            else:
                v_rows = tile(vbuf, g)
            term = _mm(a.astype(BF16), v_rows)
            tot = jnp.sum(lk, axis=1, keepdims=True)
            res = term if res is None else res + term
            run = tot if run is None else run + tot

        out = jnp.where(own, res, 0.0)
        acc = out[:, 0:SB_DH]
        for h in range(1, H):
            acc = acc + out[:, h * SB_DH:(h + 1) * SB_DH]
        so_ref[j] = acc.astype(BF16)

        if j + 2 < per_step:
            fetch(elem + 2, slot)
        else:
            @pl.when(step + 1 < n_steps)
            def _():
                fetch(elem + 2, slot)

    def interleave(i, n_iter):
        for j in range(per_step):
            if i == (j + 1) * n_iter // (per_step + 1):
                sample_sequence(j)

    _sb_prompt_body(bias_ref, q_ref, k_ref, v_ref, o_ref, kb_ref, vb_ref, interleave)


def _sb(page_table, bias, sbq, sbk, sbv, q16, k_new, v_new, cache_k, cache_v, batch, seq, layer, steps):
    dec_batch, n_pages = page_table.shape
    H = SB_HEADS
    nq = steps * H
    per_step = dec_batch // (batch * H)
    assert per_step * batch * H == dec_batch
    depth, n_phys, page_size = cache_k.shape[:3]
    cache_k = cache_k.reshape(depth, n_phys, page_size * H, SB_DH)
    cache_v = cache_v.reshape(depth, n_phys, page_size * H, SB_DH)
    past_rows = n_pages * page_size * H
    spec = pl.BlockSpec((seq, SB_DH), lambda b, h, pt: (b, h))
    kv_spec = pl.BlockSpec((1, seq * H, SB_DH), lambda b, h, pt: (layer, b, 0))
    row = pl.BlockSpec((per_step, nq, SB_DH), lambda b, h, pt: (b * H + h, 0, 0))
    new = pl.BlockSpec((per_step, steps, SB_W), lambda b, h, pt: (b * H + h, 0, 0))
    return pl.pallas_call(
        functools.partial(_sb_kernel, layer=layer, n_pages=n_pages, steps=steps),
        grid_spec=pltpu.PrefetchScalarGridSpec(
            num_scalar_prefetch=1,
            grid=(batch, H),
            in_specs=[
                pl.BlockSpec(memory_space=pltpu.SMEM),
                spec, kv_spec, kv_spec,
                row, new, new,
                pl.BlockSpec(memory_space=pl.ANY),
                pl.BlockSpec(memory_space=pl.ANY),
            ],
            out_specs=[spec, row],
            scratch_shapes=[
                pltpu.VMEM((seq, SB_DH), BF16), pltpu.VMEM((seq, SB_DH), BF16),
                pltpu.VMEM((2, past_rows, SB_DH), F32),
                pltpu.VMEM((2, past_rows, SB_DH), F32),
                pltpu.VMEM((SB_TILE, SB_W), F32),
                pltpu.VMEM((SB_TILE, SB_W), F32),
                pltpu.SemaphoreType.DMA((2, 2)),
            ],
        ),
        out_shape=[
            jax.ShapeDtypeStruct((batch * seq, SB_W), BF16),
            jax.ShapeDtypeStruct((dec_batch, nq, SB_DH), BF16),
        ],
        compiler_params=_cparams("arbitrary", "arbitrary"),
        name="sb",
    )(page_table.reshape(-1), bias, sbq, sbk, sbv, q16, k_new, v_new, cache_k, cache_v)


def _lru_kernel(x_ref, hist_ref, h0_ref, cw_ref, cb_ref, wa_ref, ba_ref, wx_ref, bx_ref, lam_ref,
                h_ref, hl_ref, tail, hc, a_s, b_s, *xt, sub):
    token_major = bool(xt)
    taps = LRU_CONV
    if token_major:
        rows, tc, width = x_ref.shape
        xt[0][...] = jnp.swapaxes(x_ref[...], 0, 1)
        x_ref = xt[0]
    else:
        tc, rows, width = x_ref.shape

    @pl.when(pl.program_id(0) == 0)
    def _():
        tail[...] = hist_ref[...]
        hc[...] = h0_ref[...]

    lam = lam_ref[...]
    log_lam = -(jnp.maximum(-lam, 0.0) + jnp.log1p(jnp.exp(-jnp.abs(lam))))
    for j in range(tc // sub):
        t0 = j * sub
        if t0 < taps - 1:
            xe = jnp.concatenate([tail[t0:taps - 1], x_ref[0:t0 + sub]], axis=0)
        else:
            xe = x_ref[t0 - (taps - 1):t0 + sub]
        xc = cb_ref[...] + xe[0:sub] * cw_ref[0]
        for i in range(1, taps):
            xc = xc + xe[i:i + sub] * cw_ref[i]
        xc = xc.reshape(sub * rows, width)
        r_parts, i_parts = [], []
        for n in range(LRU_BLOCKS):
            blk = xc[:, n * LRU_BW:(n + 1) * LRU_BW].astype(BF16)
            r_parts.append(_mm(blk, wa_ref[n]))
            i_parts.append(_mm(blk, wx_ref[n]))
        r = 0.5 * jnp.tanh(0.5 * (jnp.concatenate(r_parts, axis=1) + ba_ref[...])) + 0.5
        gate_i = 0.5 * jnp.tanh(0.5 * (jnp.concatenate(i_parts, axis=1) + bx_ref[...])) + 0.5
        log_a = LRU_C * r * log_lam
        a = jnp.exp(log_a)
        a_s[t0:t0 + sub] = a.reshape(sub, rows, width)
        one_minus_a2 = -jnp.tanh(log_a) * (a * a + 1.0)
        b_s[t0:t0 + sub] = (jnp.sqrt(one_minus_a2) * (gate_i * xc)).reshape(sub, rows, width)
    tail[...] = x_ref[tc - (taps - 1):tc]

    hist_dst = a_s if token_major else h_ref

    def step(t, h):
        h = a_s[t] * h + b_s[t]
        hist_dst[t] = h
        return h

    h = lax.fori_loop(0, tc, step, hc[...], unroll=min(tc, 8))
    hc[...] = h
    hl_ref[...] = h
    if token_major:
        h_ref[...] = jnp.swapaxes(a_s[...], 0, 1).astype(h_ref.dtype)


def _lru(x, hist, h0, cw, cb, wa, ba, wx, bx, lam, tc, token_major):
    if token_major:
        rows, steps, width = x.shape
        blk = pl.BlockSpec((rows, tc, width), lambda i: (0, i, 0))
        h_shape = jax.ShapeDtypeStruct((rows, steps, width), BF16)
    else:
        steps, rows, width = x.shape
        blk = pl.BlockSpec((tc, rows, width), lambda i: (i, 0, 0))
        h_shape = jax.ShapeDtypeStruct((steps, rows, width), F32)
    sub = max(1, LRU_ROW_CHUNK // rows)
    vec = _resident((1, width))
    time_major_buf = pltpu.VMEM((tc, rows, width), F32)
    return pl.pallas_call(
        functools.partial(_lru_kernel, sub=sub),
        grid=(steps // tc,),
        in_specs=[
            blk, _resident((LRU_CONV - 1, rows, width)), _resident((rows, width)),
            _resident((LRU_CONV, 1, width)), vec,
            _resident((LRU_BLOCKS, LRU_BW, LRU_BW)), vec,
            _resident((LRU_BLOCKS, LRU_BW, LRU_BW)), vec, vec,
        ],
        out_specs=[blk, pl.BlockSpec((rows, width), lambda i: (0, 0))],
        out_shape=[h_shape, jax.ShapeDtypeStruct((rows, width), F32)],
        scratch_shapes=[
            pltpu.VMEM((LRU_CONV - 1, rows, width), F32),
            pltpu.VMEM((rows, width), F32),
            time_major_buf, time_major_buf,
        ] + ([time_major_buf] if token_major else []),
        compiler_params=_cparams("arbitrary"),
        name="lru",
    )(x, hist, h0, cw, cb, wa, ba, wx, bx, lam)


def _merge_kernel(x_ref, oret_ref, osb_ref, olru_ref, g_ref, wgl_ref, wr_ref, ws_ref, wl_ref, wo_ref,
                  out_ref, m_ref):
    x = x_ref[...]
    xn = _rms(x, g_ref[...]).astype(BF16)
    branches = ((oret_ref, wr_ref), (osb_ref, ws_ref), (olru_ref, wl_ref))
    cw = 256
    for c in range(D_MODEL // cw):
        cols = slice(c * cw, (c + 1) * cw)
        m = None
        for j, (o_ref, w_ref) in enumerate(branches):
            lo = PROJ_COLS + j * D_MODEL + c * cw
            gate = jax.nn.sigmoid(_mm(xn, wgl_ref[0, :, lo:lo + cw]))
            term = gate * _mm(o_ref[...], w_ref[0, :, cols])
            m = term if m is None else m + term
        m_ref[:, cols] = m.astype(BF16)
    out_ref[...] = x + _mm(m_ref[...], wo_ref[0])


def _merge(x, o_ret, o_sb, o_lru, g, w_in, wr, ws, wl, wo, layer):
    n = x.shape[0]
    tm = TOKEN_TILE
    row = lambda width: pl.BlockSpec((tm, width), lambda i: (i, 0))
    return pl.pallas_call(
        _merge_kernel,
        grid=(n // tm,),
        in_specs=[
            row(D_MODEL), row(RET_W), row(SB_W), row(LRU_WIDTH),
            _resident((1, D_MODEL)), _layer_block(w_in, layer),
            _layer_block(wr, layer), _layer_block(ws, layer), _layer_block(wl, layer),
            _layer_block(wo, layer),
        ],
        out_specs=row(D_MODEL),
        out_shape=jax.ShapeDtypeStruct((n, D_MODEL), F32),
        scratch_shapes=[pltpu.VMEM((tm, D_MODEL), BF16)],
        compiler_params=_cparams("parallel"),
        name="merge",
    )(x, o_ret, o_sb, o_lru, g, w_in, wr, ws, wl, wo)


def _ffn_kernel(x_ref, g_ref, wg_ref, wu_ref, cw_ref, cb_ref, wd_ref, hist_ref, nf_ref, out_ref, st_ref,
                carry, h_ref, *, shift, tiles_per_seq, final_norm):
    tm = x_ref.shape[0]
    hist_rows = carry.shape[0]
    i = pl.program_id(0)

    @pl.when(i % tiles_per_seq == 0)
    def _():
        carry[...] = hist_ref[...]

    x = x_ref[...]
    xn = _rms(x, g_ref[...]).astype(BF16)
    S = V7X_SUBLANES
    n_chunks = D_FF // FFN_COL_CHUNK
    col_slice = lambda c: slice(c * FFN_COL_CHUNK, (c + 1) * FFN_COL_CHUNK)

    def up(c):
        return _mm(xn, wg_ref[0, :, col_slice(c)]), _mm(xn, wu_ref[0, :, col_slice(c)])

    def act(c, g, u):
        cols = col_slice(c)
        hist = carry[:, cols]
        if shift == 1:
            ext = jnp.concatenate([hist, g[0:S]], axis=0)
            p1 = jnp.concatenate([pltpu.roll(ext, 1, 0)[S:2 * S], pltpu.roll(g, 1, 0)[S:]], axis=0)
            p2 = jnp.concatenate([pltpu.roll(ext, 2, 0)[S:2 * S], pltpu.roll(g, 2, 0)[S:]], axis=0)
        else:
            p1 = jnp.concatenate([hist[shift:2 * shift], g[0:tm - shift]], axis=0)
            p2 = jnp.concatenate([hist[0:2 * shift], g[0:tm - 2 * shift]], axis=0)
        carry[:, cols] = g[tm - hist_rows:tm]
        gc = cb_ref[:, cols] + p2 * cw_ref[0, :, cols] + p1 * cw_ref[1, :, cols] + g * cw_ref[2, :, cols]
        return (jax.nn.gelu(gc) * u).astype(BF16)

    ups = {}
    for c in range(n_chunks + 1):
        if c < n_chunks:
            ups[c] = up(c)
        if c >= 1:
            h_ref[:, col_slice(c - 1)] = act(c - 1, *ups.pop(c - 1))
    st_ref[...] = carry[hist_rows - 2 * shift:hist_rows, :].reshape(st_ref.shape)
    y = x + _mm(h_ref[...], wd_ref[0])
    if final_norm:
        y = _rms(y, nf_ref[...])
    out_ref[...] = y


def _ffn(x, g, wg, wu, cw, cb, wd, hist, nf, layer, shift, tiles_per_seq, final_norm):
    n = x.shape[0]
    tm = TOKEN_TILE
    n_seq = n // (tm * tiles_per_seq)
    hist_rows = hist.shape[0]
    row = pl.BlockSpec((tm, D_MODEL), lambda i: (i, 0))
    return pl.pallas_call(
        functools.partial(_ffn_kernel, shift=shift, tiles_per_seq=tiles_per_seq, final_norm=final_norm),
        grid=(n // tm,),
        in_specs=[
            row, _resident((1, D_MODEL)),
            _layer_block(wg, layer), _layer_block(wu, layer),
            _resident((FFN_CONV, 1, D_FF)), _resident((1, D_FF)),
            _layer_block(wd, layer),
            _resident((hist_rows, D_FF)), _resident((1, D_MODEL)),
        ],
        out_specs=[row, pl.BlockSpec((1, 2 * shift, D_FF), lambda i: (i // tiles_per_seq, 0, 0))],
        out_shape=[
            jax.ShapeDtypeStruct((n, D_MODEL), F32),
            jax.ShapeDtypeStruct((n_seq, 2 * shift, D_FF), F32),
        ],
        scratch_shapes=[pltpu.VMEM((hist_rows, D_FF), F32), pltpu.VMEM((tm, D_FF), BF16)],
        compiler_params=_cparams("arbitrary"),
        name="ffn",
    )(x, g, wg, wu, cw, cb, wd, hist, nf)


def _rot_tables(pos):
    half = RET_DK // 2
    inv = ROPE_BASE ** (-jnp.arange(half, dtype=F32) / half)
    ang = pos[:, None] * inv[None, :]
    c, s = jnp.cos(ang), jnp.sin(ang)
    return jnp.concatenate([c, c], axis=-1), jnp.concatenate([-s, s], axis=-1)


def _layer_weights(l, p):
    return dict(
        layer=l,
        norm1=p["norm1"][l][None, :],
        w_in=p["w_in"],
        ret_gn=p["ret_gn"][l][None, :],
        sb_bias=p["sb_bias"][l],
        lru_cw=p["lru_conv_w"][l][:, None, :],
        lru_cb=p["lru_conv_b"][l][None, :],
        lru_wa=p["lru_w_a"][l].astype(BF16),
        lru_ba=p["lru_b_a"][l][None, :],
        lru_wx=p["lru_w_x"][l].astype(BF16),
        lru_bx=p["lru_b_x"][l][None, :],
        lru_lam=p["lru_lambda"][l][None, :],
        w_br_ret=p["w_br_ret"], w_br_sb=p["w_br_sb"], w_br_lru=p["w_br_lru"], w_out=p["w_out"],
        norm2=p["norm2"][l][None, :],
        w_ffn_gate=p["w_ffn_gate"], w_ffn_up=p["w_ffn_up"],
        ffn_cw=p["ffn_conv_w"][l][:, None, :],
        ffn_cb=p["ffn_conv_b"][l][None, :],
        w_ffn_down=p["w_ffn_down"],
        norm_f=p["norm_f"][None, :],
    )


def _mix_and_ffn(x, o_ret, o_sb, o_lru, lw, ffn_hist, shift, tiles_per_seq, final_norm):
    x1 = _merge(x, o_ret, o_sb, o_lru, lw["norm1"], lw["w_in"], lw["w_br_ret"], lw["w_br_sb"],
                lw["w_br_lru"], lw["w_out"], lw["layer"])
    return _ffn(x1, lw["norm2"], lw["w_ffn_gate"], lw["w_ffn_up"], lw["ffn_cw"], lw["ffn_cb"],
                lw["w_ffn_down"], ffn_hist, lw["norm_f"], lw["layer"], shift, tiles_per_seq, final_norm)


def _layer(xp, xs, lw, lg, tables, batch, seq, dec_batch, steps, layer, depth, kv_stack, page_table, cache_k,
           cache_v, state_ret, ret_out, state_lru_h, state_lru_conv, state_ffn_conv, final_norm):
    (cos_p, sin_p), (cos_s, sin_s) = tables
    heads = lambda a: a.reshape(steps, dec_batch, SB_HEADS, SB_DH).transpose(1, 0, 2, 3)
    p_proj = _proj(xp, lw["norm1"], lw["w_in"], layer, cos_p, sin_p, stacked=(depth,) + kv_stack)
    s_proj = _proj(xs, lw["norm1"], lw["w_in"], layer, cos_s, sin_s)
    sbq_p, sbk_p, sbv_p = p_proj[2:5]
    sbq_s, sbk_s, sbv_s = s_proj[2:5]
    k_new, v_new = heads(sbk_s), heads(sbv_s)
    o_sb_p, o16 = _sb(page_table, lw["sb_bias"], sbq_p, sbk_p, sbv_p,
                      heads(sbq_s).reshape(dec_batch, steps * SB_HEADS, SB_DH),
                      k_new.reshape(dec_batch, steps, SB_W), v_new.reshape(dec_batch, steps, SB_W),
                      cache_k, cache_v, batch, seq, layer, steps)
    xp, st_p = _prompt_rest(xp, p_proj, o_sb_p, lw, lg, batch, seq, final_norm)
    xs, ret_out, st_s = _sample_rest(xs, s_proj, o16, k_new, v_new, lw, lg, dec_batch, steps, layer, state_ret,
                                     ret_out, state_lru_h, state_lru_conv, state_ffn_conv, final_norm)
    return xp, xs, (sbk_p, sbv_p), ret_out, st_p, st_s


def _prompt_rest(x, proj, o_sb, lw, lg, batch, seq, final_norm):
    retq, ret, _, _, _, lx = proj
    o_ret, s_new = _ret_prompt(retq, ret, lg, lw["ret_gn"], batch, seq)
    lx3 = lx.reshape(batch, seq, LRU_WIDTH)
    h, h_last = _lru(
        lx3, jnp.zeros((LRU_CONV - 1, batch, LRU_WIDTH), F32), jnp.zeros((batch, LRU_WIDTH), F32),
        lw["lru_cw"], lw["lru_cb"], lw["lru_wa"], lw["lru_ba"], lw["lru_wx"], lw["lru_bx"], lw["lru_lam"],
        tc=256, token_major=True)
    o_lru = h.reshape(batch * seq, LRU_WIDTH)
    x2, ffn_state = _mix_and_ffn(x, o_ret, o_sb, o_lru, lw, jnp.zeros((V7X_SUBLANES, D_FF), F32),
                                 shift=1, tiles_per_seq=seq // TOKEN_TILE, final_norm=final_norm)
    states = (s_new, h_last, lx3[:, seq - (LRU_CONV - 1):, :], ffn_state)
    return x2, states


def _sample_rest(x, proj, o16, k_new, v_new, lw, lg, batch, steps, layer, state_ret, ret_out, state_lru_h,
                 state_lru_conv, state_ffn_conv, final_norm):
    to_tok = lambda a: a.reshape(steps, batch, -1).transpose(1, 0, 2).reshape(batch * steps, -1)
    to_tm = lambda a: a.reshape(batch, steps, -1).transpose(1, 0, 2).reshape(steps * batch, -1)
    retq, ret, _, _, _, lx = proj
    o_ret_tok, ret_out = _ret_sample(to_tok(retq), to_tok(ret), lg, lw["ret_gn"], state_ret, steps, layer,
                                     ret_out)
    o_ret = to_tm(o_ret_tok)
    o_sb = to_tm(o16.reshape(batch * steps, SB_W))
    lx3 = lx.reshape(steps, batch, LRU_WIDTH)
    h_tm, h_last = _lru(
        lx3, state_lru_conv.transpose(1, 0, 2), state_lru_h,
        lw["lru_cw"], lw["lru_cb"], lw["lru_wa"], lw["lru_ba"], lw["lru_wx"], lw["lru_bx"], lw["lru_lam"],
        tc=steps, token_major=False)
    o_lru = h_tm.reshape(steps * batch, LRU_WIDTH).astype(BF16)
    ffn_hist = state_ffn_conv.transpose(1, 0, 2).reshape((FFN_CONV - 1) * batch, D_FF)
    x2, ffn_state = _mix_and_ffn(x, o_ret, o_sb, o_lru, lw, ffn_hist, shift=batch, tiles_per_seq=1,
                                 final_norm=final_norm)
    states = (
        k_new, v_new, h_last,
        lx3[steps - (LRU_CONV - 1):].transpose(1, 0, 2),
        ffn_state.reshape(FFN_CONV - 1, batch, D_FF).transpose(1, 0, 2),
    )
    return x2, ret_out, states


def kernel(x_prompt, x_sample, cache_sb_k, cache_sb_v, state_ret, state_lru_h, state_lru_conv, state_ffn_conv,
           page_table, norm1, w_in, ret_gn, sb_bias, lru_conv_w, lru_conv_b, lru_w_a, lru_b_a, lru_w_x, lru_b_x,
           lru_lambda, w_br_ret, w_br_sb, w_br_lru, w_out, norm2, w_ffn_gate, w_ffn_up, ffn_conv_w, ffn_conv_b,
           w_ffn_down, norm_f):
    bf = lambda w: w.astype(BF16)
    params = dict(norm1=norm1, w_in=bf(w_in), ret_gn=ret_gn, sb_bias=sb_bias, lru_conv_w=lru_conv_w,
                  lru_conv_b=lru_conv_b, lru_w_a=lru_w_a, lru_b_a=lru_b_a, lru_w_x=lru_w_x, lru_b_x=lru_b_x,
                  lru_lambda=lru_lambda, w_br_ret=bf(w_br_ret), w_br_sb=bf(w_br_sb), w_br_lru=bf(w_br_lru),
                  w_out=bf(w_out), norm2=norm2, w_ffn_gate=bf(w_ffn_gate), w_ffn_up=bf(w_ffn_up),
                  ffn_conv_w=ffn_conv_w, ffn_conv_b=ffn_conv_b, w_ffn_down=bf(w_ffn_down), norm_f=norm_f)
    depth = w_in.shape[0]
    batch, seq, _ = x_prompt.shape
    dec_batch, steps, _ = x_sample.shape
    past_len = page_table.shape[1] * cache_sb_k.shape[2]
    assert seq % TOKEN_TILE == 0 and dec_batch * steps == TOKEN_TILE and steps & (steps - 1) == 0
    assert steps >= LRU_CONV - 1 and steps >= FFN_CONV - 1

    lg = jnp.log(1.0 - 2.0 ** (-5.0 - jnp.arange(RET_HEADS, dtype=F32)))
    cos_p, sin_p = _rot_tables(jnp.arange(seq, dtype=F32))
    cos_s, sin_s = _rot_tables(past_len + jnp.repeat(jnp.arange(steps, dtype=F32), dec_batch))

    xp = x_prompt.reshape(batch * seq, D_MODEL)
    xs = x_sample.transpose(1, 0, 2).reshape(steps * dec_batch, D_MODEL)
    st_p, st_s = [], []
    kv_stack = (None, None)
    ret_sample = None
    for l in range(depth):
        lw = _layer_weights(l, params)
        last = l == depth - 1
        xp, xs, kv_stack, ret_sample, st_prompt, st_sample = _layer(
            xp, xs, lw, lg, ((cos_p, sin_p), (cos_s, sin_s)), batch, seq, dec_batch, steps, l, depth, kv_stack,
            page_table, cache_sb_k, cache_sb_v, state_ret, ret_sample, state_lru_h[l], state_lru_conv[l],
            state_ffn_conv[l], last)
        st_p.append(st_prompt)
        st_s.append(st_sample)
    y_prompt = xp.reshape(batch, seq, D_MODEL)
    y_sample = xs.reshape(steps, dec_batch, D_MODEL).transpose(1, 0, 2)
    k_prompt, v_prompt = (a.reshape(depth, batch, seq, SB_HEADS, SB_DH) for a in kv_stack)
    stack = lambda sts, j: jnp.stack([s[j] for s in sts], axis=0)
    return (y_prompt, y_sample,
            k_prompt, v_prompt, stack(st_s, 0), stack(st_s, 1),
            stack(st_p, 0), ret_sample,
            stack(st_p, 1), stack(st_s, 2),
            stack(st_p, 2), stack(st_s, 3),
            stack(st_p, 3), stack(st_s, 4))
```

```python
import functools

import jax
import jax.numpy as jnp
from jax import lax
from jax.experimental import pallas as pl
from jax.experimental.pallas import tpu as pltpu

F32 = jnp.float32
BF16 = jnp.bfloat16

D_MODEL = 1024
RET_HEADS = 4
RET_DK = 128
RET_DV = 128
RET_CHUNK = 128
ROPE_BASE = 10000.0
SB_HEADS = 4
SB_DH = 128
SB_TILE = 256
SB_Q_ROWS = 256
LRU_WIDTH = 512
LRU_BLOCKS = 4
LRU_BW = LRU_WIDTH // LRU_BLOCKS
LRU_CONV = 4
LRU_C = 8.0
N_BRANCH = 3
D_FF = 2816
FFN_CONV = 3
EPS = 1e-6

RET_W = RET_HEADS * RET_DK
SB_W = SB_HEADS * SB_DH
PROJ_COLS = 4 * RET_W + 3 * SB_W + LRU_WIDTH

V7X_LANES = 128
V7X_SUBLANES = 8
V7X_VMEM_LIMIT_BYTES = 60000 * 1024

TOKEN_TILE = 512
FFN_COL_CHUNK = 256
LRU_ROW_CHUNK = 256


def _cparams(*sem):
    return pltpu.CompilerParams(dimension_semantics=sem, vmem_limit_bytes=V7X_VMEM_LIMIT_BYTES)


def _resident(shape):
    zeros = (0,) * len(shape)
    return pl.BlockSpec(shape, lambda *_: zeros, pipeline_mode=pl.Buffered(1))


def _layer_block(w, layer):
    zeros = (0,) * (w.ndim - 1)
    return pl.BlockSpec((1,) + w.shape[1:], lambda *_: (layer,) + zeros, pipeline_mode=pl.Buffered(1))


def _mm(a, b):
    return jnp.dot(a, b, preferred_element_type=F32)


def _mm_nt(a, b):
    return lax.dot_general(a, b, (((1,), (1,)), ((), ())), preferred_element_type=F32)


def _mm_tn(a, b):
    return lax.dot_general(a, b, (((0,), (0,)), ((), ())), preferred_element_type=F32)


def _rms(x, g):
    return x * lax.rsqrt(jnp.mean(x * x, axis=-1, keepdims=True) + EPS) * g


LOG2E = 1.4426950408889634


def _neg_abs(x):
    bits = lax.bitcast_convert_type(x, jnp.uint32) | jnp.uint32(0x80000000)
    return lax.bitcast_convert_type(bits, F32)


def _softplus(z):
    return jnp.maximum(z, 0.0) + jnp.log(1.0 + jnp.exp(-jnp.abs(z)))


def _suffix_sum(lk, tri):
    hi = lk.astype(BF16)
    lo = (lk - hi.astype(F32)).astype(BF16)
    if tri.shape[0] == 2 * lk.shape[1]:
        return _mm(jnp.concatenate([hi, lo], axis=1), tri)
    return _mm(hi, tri) + _mm(lo, tri)


def _strict_lower(n):
    r = lax.broadcasted_iota(jnp.int32, (n, n), 0)
    c = lax.broadcasted_iota(jnp.int32, (n, n), 1)
    return (r > c).astype(BF16)


def _proj_kernel(x_ref, g_ref, w_ref, cos_ref, sin_ref, retq_ref, ret_ref, sbq_ref, sbk_ref, sbv_ref, lx_ref):
    w = lambda lo, width: w_ref[0, :, lo:lo + width]
    xn = _rms(x_ref[...], g_ref[...]).astype(BF16)
    cs, sn = cos_ref[...], sin_ref[...]
    rq = _mm(xn, w(0, RET_W))
    rk = _mm(xn, w(RET_W, RET_W))
    for h in range(RET_HEADS):
        sl = slice(h * RET_DK, (h + 1) * RET_DK)
        retq_ref[:, sl] = _rotary(rq[:, sl], cs, sn).astype(BF16)
        ret_ref[:, sl] = _rotary(rk[:, sl], cs, sn) * (RET_DK ** -0.5)
    for c in range(2, 4):
        ret_ref[:, (c - 1) * RET_W:c * RET_W] = _mm(xn, w(c * RET_W, RET_W))
    o = 4 * RET_W
    sbq_ref[...] = _mm(xn, w(o, SB_W)).astype(BF16)
    lx_ref[...] = _mm(xn, w(o + 3 * SB_W, LRU_WIDTH))
    for ref, lo in ((sbk_ref, o + SB_W), (sbv_ref, o + 2 * SB_W)):
        kv = _mm(xn, w(lo, SB_W))
        if len(ref.shape) == 2:
            ref[...] = kv
        else:
            tm = kv.shape[0]
            for h in range(SB_HEADS):
                ref[0, pl.ds(h, tm, stride=SB_HEADS), :] = kv[:, h * SB_DH:(h + 1) * SB_DH]
            for later in range(1, ref.shape[0]):
                ref[later] = jnp.zeros(ref.shape[1:], F32)


def _proj_kernel_stacked(x_ref, g_ref, w_ref, cos_ref, sin_ref, prev_k, prev_v, *out_refs):
    _proj_kernel(x_ref, g_ref, w_ref, cos_ref, sin_ref, *out_refs)


def _proj(x, g, w_in, layer, cos, sin, stacked=None):
    n = x.shape[0]
    tm = TOKEN_TILE
    row = lambda width: pl.BlockSpec((tm, width), lambda i: (i, 0))
    period = cos.shape[0] // tm
    table = pl.BlockSpec((tm, RET_DK), lambda i: (i % period, 0))
    in_specs = [row(D_MODEL), _resident((1, D_MODEL)), _layer_block(w_in, layer), table, table]
    args = [x, g, w_in, cos, sin]
    body, aliases = _proj_kernel, {}
    if stacked is None:
        kv_spec, kv_shape = row(SB_W), jax.ShapeDtypeStruct((n, SB_W), F32)
    else:
        depth, prev_k, prev_v = stacked
        kv_shape = jax.ShapeDtypeStruct((depth, n * SB_HEADS, SB_DH), F32)
        if prev_k is None:
            assert layer == 0
            kv_spec = pl.BlockSpec((depth, tm * SB_HEADS, SB_DH), lambda i: (0, i, 0))
        else:
            kv_spec = pl.BlockSpec((1, tm * SB_HEADS, SB_DH), lambda i: (layer, i, 0))
            body, aliases = _proj_kernel_stacked, {5: 3, 6: 4}
            in_specs += [pl.BlockSpec(memory_space=pl.ANY)] * 2
            args += [prev_k, prev_v]
    return pl.pallas_call(
        body,
        grid=(n // tm,),
        in_specs=in_specs,
        out_specs=[row(RET_W), row(3 * RET_W), row(SB_W), kv_spec, kv_spec, row(LRU_WIDTH)],
        out_shape=[
            jax.ShapeDtypeStruct((n, RET_W), BF16),
            jax.ShapeDtypeStruct((n, 3 * RET_W), F32),
            jax.ShapeDtypeStruct((n, SB_W), BF16),
            kv_shape, kv_shape,
            jax.ShapeDtypeStruct((n, LRU_WIDTH), F32),
        ],
        input_output_aliases=aliases,
        compiler_params=_cparams("parallel"),
        name="proj",
    )(*args)


def _rotary(x, cs, sn):
    return x * cs + pltpu.roll(x, RET_DK // 2, 1) * sn


def _group_norm_gate(o, gn, gate):
    mu = jnp.mean(o, axis=-1, keepdims=True)
    d = o - mu
    var = jnp.mean(d * d, axis=-1, keepdims=True)
    return d * lax.rsqrt(var + EPS) * gn * (gate * jax.nn.sigmoid(gate))


def _ret_prompt_kernel(lg_ref, q_ref, k_ref, v_ref, g_ref, gn_ref, o_ref, s_ref):
    lg = lg_ref[pl.program_id(1)]
    L = RET_CHUNK
    n_chunks = q_ref.shape[0] // L
    row = lax.broadcasted_iota(jnp.int32, (L, L), 0).astype(F32)
    col = lax.broadcasted_iota(jnp.int32, (L, L), 1).astype(F32)
    diff = row - col
    dmat = jnp.where(diff >= 0, jnp.exp(jnp.maximum(diff, 0.0) * lg), 0.0)
    q_dec = jnp.exp((row + 1.0) * lg)
    k_dec = jnp.exp((L - 1.0 - row) * lg)
    s_dec = jnp.exp(jnp.full((RET_DK, RET_DV), float(L), F32) * lg)
    gn = gn_ref[...]

    rows_of = lambda c: slice(c * L, (c + 1) * L)

    def local(c):
        rows = rows_of(c)
        qb, k = q_ref[rows, :], k_ref[rows, :]
        vb = v_ref[rows, :].astype(BF16)
        scores = (_mm_nt(qb, k.astype(BF16)) * dmat).astype(BF16)
        return qb, vb, scores, _mm_tn((k * k_dec).astype(BF16), vb)

    s = jnp.zeros((RET_DK, RET_DV), F32)

    def attend(qb, vb, scores, upd):
        nonlocal s
        o = _mm(scores, vb) + _mm(qb, s.astype(BF16)) * q_dec
        s = s_dec * s + upd
        return o

    def emit(c, o):
        rows = rows_of(c)
        o_ref[rows, :] = _group_norm_gate(o, gn, g_ref[rows, :]).astype(BF16)

    locs, outs = {}, {}
    for c in range(n_chunks + 2):
        if c < n_chunks:
            locs[c] = local(c)
        if 1 <= c <= n_chunks:
            outs[c - 1] = attend(*locs.pop(c - 1))
        if c >= 2:
            emit(c - 2, outs.pop(c - 2))
    s_ref[0, 0] = s


def _ret_prompt(retq, ret, lg, gn, batch, seq):
    H = RET_HEADS
    col = lambda j: pl.BlockSpec((seq, RET_DK), lambda b, h, j=j: (b, j * H + h))
    return pl.pallas_call(
        _ret_prompt_kernel,
        grid=(batch, H),
        in_specs=[
            pl.BlockSpec(memory_space=pltpu.SMEM),
            col(0), col(0), col(1), col(2),
            pl.BlockSpec((1, RET_DV), lambda b, h: (0, h)),
        ],
        out_specs=[
            pl.BlockSpec((seq, RET_DV), lambda b, h: (b, h)),
            pl.BlockSpec((1, 1, RET_DK, RET_DV), lambda b, h: (b, h, 0, 0)),
        ],
        out_shape=[
            jax.ShapeDtypeStruct((batch * seq, RET_W), BF16),
            jax.ShapeDtypeStruct((batch, H, RET_DK, RET_DV), F32),
        ],
        compiler_params=_cparams("parallel", "parallel"),
        name="ret_prompt",
    )(lg, retq, ret, ret, ret, gn)


def _ret_sample_kernel(lg_ref, q_ref, k_ref, v_ref, g_ref, gn_ref, st_ref, o_ref, so_ref, *, nb, steps):
    R = nb * steps
    P = RET_DK
    shift = steps.bit_length() - 1
    ri = lax.broadcasted_iota(jnp.int32, (R, P), 0)
    ci = lax.broadcasted_iota(jnp.int32, (R, P), 1)
    rb, rt = ri >> shift, ri & (steps - 1)
    cb, ct = ci >> shift, ci & (steps - 1)
    same = (rb == cb) & (ci < R) & (ct <= rt)
    dt = jnp.maximum(rt - ct, 0).astype(F32)
    t_row = rt.astype(F32)
    wide_r = lax.broadcasted_iota(jnp.int32, (R, nb * RET_DV), 0) >> shift
    wide_c = lax.broadcasted_iota(jnp.int32, (R, nb * RET_DV), 1) >> (RET_DV.bit_length() - 1)
    own = wide_r == wide_c
    zpad = jnp.zeros((P - R, RET_DK), BF16)
    zpad_w = jnp.zeros((P - R, nb * RET_DV), BF16)
    for h in range(RET_HEADS):
        lg = lg_ref[h]
        sl = slice(h * RET_DK, (h + 1) * RET_DK)
        dmat = jnp.where(same, jnp.exp(dt * lg), 0.0)
        q_dec = jnp.exp((t_row + 1.0) * lg)
        k_dec = jnp.exp((steps - 1.0 - t_row) * lg)
        s_dec = jnp.exp(jnp.full((RET_DK, RET_DV), float(steps), F32) * lg)
        qb, k, v = q_ref[:, sl], k_ref[:, sl], v_ref[:, sl]
        k_pad = jnp.concatenate([k.astype(BF16), zpad], axis=0)
        v_pad = jnp.concatenate([v.astype(BF16), zpad], axis=0)
        scores = _mm_nt(qb, k_pad) * dmat
        intra = _mm(scores.astype(BF16), v_pad)
        s_cat = jnp.concatenate([st_ref[0, b, h] for b in range(nb)], axis=1)
        wide = jnp.where(own, _mm(qb, s_cat.astype(BF16)), 0.0)
        cross = wide[:, 0:RET_DV]
        for b in range(1, nb):
            cross = cross + wide[:, b * RET_DV:(b + 1) * RET_DV]
        o = intra + cross * q_dec
        o_ref[:, sl] = _group_norm_gate(o, gn_ref[:, sl], g_ref[:, sl]).astype(BF16)
        kd_pad = jnp.concatenate([(k * k_dec).astype(BF16), zpad], axis=0)
        v_wide = jnp.where(own, jnp.concatenate([v] * nb, axis=1), 0.0).astype(BF16)
        upd = _mm_tn(kd_pad, jnp.concatenate([v_wide, zpad_w], axis=0))
        for b in range(nb):
            so_ref[0, b, h] = s_dec * st_ref[0, b, h] + upd[:, b * RET_DV:(b + 1) * RET_DV]
    for later in range(1, so_ref.shape[0]):
        so_ref[later] = jnp.zeros(so_ref.shape[1:], F32)


def _ret_sample_kernel_stacked(*refs, nb, steps):
    _ret_sample_kernel(*refs[:7], *refs[8:], nb=nb, steps=steps)


def _ret_sample(retq_tok, ret_tok, lg, gn, states, steps, layer, prev_out):
    n = ret_tok.shape[0]
    batch = n // steps
    nb = 8
    R = nb * steps
    col = lambda j: pl.BlockSpec((R, RET_W), lambda i, j=j: (i, j))
    st_spec = pl.BlockSpec((1, nb, RET_HEADS, RET_DK, RET_DV), lambda i: (layer, i, 0, 0, 0))
    in_specs = [
        pl.BlockSpec(memory_space=pltpu.SMEM),
        col(0), col(0), col(1), col(2),
        _resident((1, RET_W)),
        st_spec,
    ]
    args = [lg, retq_tok, ret_tok, ret_tok, ret_tok, gn, states]
    body, aliases = _ret_sample_kernel, {}
    if prev_out is None:
        assert layer == 0
        out_spec = pl.BlockSpec((states.shape[0], nb, RET_HEADS, RET_DK, RET_DV), lambda i: (0, i, 0, 0, 0))
    else:
        out_spec = st_spec
        body, aliases = _ret_sample_kernel_stacked, {7: 1}
        in_specs.append(pl.BlockSpec(memory_space=pl.ANY))
        args.append(prev_out)
    return pl.pallas_call(
        functools.partial(body, nb=nb, steps=steps),
        grid=(batch // nb,),
        in_specs=in_specs,
        out_specs=[pl.BlockSpec((R, RET_W), lambda i: (i, 0)), out_spec],
        out_shape=[
            jax.ShapeDtypeStruct((n, RET_W), BF16),
            jax.ShapeDtypeStruct(states.shape, F32),
        ],
        input_output_aliases=aliases,
        compiler_params=_cparams("parallel"),
        name="ret_sample",
    )(*args)


def _sb_prompt_body(bias_ref, q_ref, k_ref, v_ref, o_ref, kb_ref, vb_ref, interleave):
    head = pl.program_id(1)
    bias = bias_ref[head]
    n = SB_TILE
    m = SB_Q_ROWS
    seq = q_ref.shape[0]
    scale = SB_DH ** -0.5
    kb_ref[...] = k_ref[0, pl.ds(head, seq, stride=SB_HEADS), :].astype(BF16)
    vb_ref[...] = v_ref[0, pl.ds(head, seq, stride=SB_HEADS), :].astype(BF16)
    tri = _strict_lower(n)
    tri2 = jnp.concatenate([tri, tri], axis=0)
    row = lax.broadcasted_iota(jnp.int32, (m, n), 0)
    col = lax.broadcasted_iota(jnp.int32, (m, n), 1)
    causal_at = [col < row + off for off in range(0, n, m)]

    tiles = []
    for qi in range(seq // m):
        last = qi * m // n
        tiles += [(qi, c, causal_at[qi * m % n // m] if c == last else None, c == last, c == 0)
                  for c in range(last, -1, -1)]

    scale2, bias2 = scale * LOG2E, bias * LOG2E

    def logits(qi, c, diag, first, final):
        return _mm_nt(q_ref[qi * m:(qi + 1) * m, :], kb_ref[c * n:(c + 1) * n, :]) * scale2 + bias2

    def keep(qi, c, diag, first, final, z):
        sp = jnp.maximum(z, 0.0) + jnp.log(1.0 + jnp.exp2(_neg_abs(z))) * LOG2E
        drop = sp if diag is None else jnp.where(diag, sp, 0.0)
        return z - sp, _suffix_sum(drop, tri2), jnp.sum(drop, axis=1, keepdims=True)

    acc = run = None

    def weigh(qi, c, diag, first, final, ls, suffix, tot):
        nonlocal run
        e = ls - suffix
        a = jnp.exp2(e if first else e - run)
        if diag is not None:
            a = jnp.where(diag, a, 0.0)
        run = tot if first else run + tot
        return _mm(a.astype(BF16), vb_ref[c * n:(c + 1) * n, :])

    def collect(qi, c, diag, first, final, term):
        nonlocal acc
        acc = term if first else acc + term
        if final:
            o_ref[qi * m:(qi + 1) * m, :] = acc.astype(BF16)

    n_tiles = len(tiles)
    zs, ks, ws = {}, {}, {}
    for i in range(n_tiles + 3):
        if i < n_tiles:
            zs[i] = logits(*tiles[i])
        if 1 <= i <= n_tiles:
            ks[i - 1] = keep(*tiles[i - 1], zs.pop(i - 1))
        if 2 <= i <= n_tiles + 1:
            ws[i - 2] = weigh(*tiles[i - 2], *ks.pop(i - 2))
        if i >= 3:
            collect(*tiles[i - 3], ws.pop(i - 3))
        interleave(i, n_tiles + 3)


def _sb_kernel(pt_ref, bias_ref, q_ref, k_ref, v_ref, sq_ref, kn_ref, vn_ref, k_hbm, v_hbm, o_ref, so_ref,
               kb_ref, vb_ref, kbuf, vbuf, newk, newv, sem, *, layer, n_pages, steps):
    per_step = sq_ref.shape[0]
    assert per_step % 2 == 0
    step = pl.program_id(0) * pl.num_programs(1) + pl.program_id(1)
    n_steps = pl.num_programs(0) * pl.num_programs(1)
    H = SB_HEADS
    hbits = H.bit_length() - 1
    nq = steps * H
    page_rows = k_hbm.shape[2]
    n = SB_TILE
    n_chunks = n_pages * page_rows // (H * n)
    scale = SB_DH ** -0.5

    def copies(elem, slot):
        out = []
        for p in range(n_pages):
            pg = pt_ref[elem * n_pages + p]
            rows = pl.ds(p * page_rows, page_rows)
            out.append(pltpu.make_async_copy(k_hbm.at[layer, pg], kbuf.at[slot, rows, :], sem.at[0, slot]))
            out.append(pltpu.make_async_copy(v_hbm.at[layer, pg], vbuf.at[slot, rows, :], sem.at[1, slot]))
        return out

    def fetch(elem, slot):
        for c in copies(elem, slot):
            c.start()

    @pl.when(step == 0)
    def _():
        fetch(0, 0)
        fetch(1, 1)
        newk[...] = jnp.zeros(newk.shape, F32)
        newv[...] = jnp.zeros(newv.shape, F32)

    rw = lax.broadcasted_iota(jnp.int32, (nq, SB_W), 0)
    cw = lax.broadcasted_iota(jnp.int32, (nq, SB_W), 1)
    own = (rw & (H - 1)) == (cw >> (SB_DH.bit_length() - 1))
    rn = lax.broadcasted_iota(jnp.int32, (nq, n), 0)
    cn = lax.broadcasted_iota(jnp.int32, (nq, n), 1)
    rh = rn & (H - 1)
    bias = jnp.where(rh == 0, bias_ref[0], jnp.where(rh == 1, bias_ref[1],
                     jnp.where(rh == 2, bias_ref[2], bias_ref[3])))
    tri = _strict_lower(n)
    valid = cn < (rn >> hbits)

    def sample_sequence(j):
        elem = step * per_step + j
        slot = j % 2

        def tile(buf, g):
            heads = [buf[slot, pl.ds(g * n * H + h, n, stride=H), :] for h in range(H)]
            return jnp.concatenate(heads, axis=1).astype(BF16)

        wq = jnp.where(own, jnp.concatenate([sq_ref[j]] * H, axis=1), jnp.zeros((nq, SB_W), BF16))

        def scores(k_rows):
            z = _mm_nt(wq, k_rows) * scale + bias
            return z, _softplus(z)

        newk[0:steps, :] = kn_ref[j]
        newv[0:steps, :] = vn_ref[j]
        z, sp = scores(newk[...].astype(BF16))
        parts = [(z, sp, jnp.where(valid, -sp, 0.0), None)]

        for c in copies(elem, slot):
            c.wait()
        yield

        for g in range(n_chunks - 1, -1, -1):
            z, sp = scores(tile(kbuf, g))
            parts.append((z, sp, -sp, g))
            if g % 2 == 0:
                yield

        split = []
        for _, _, lk, _ in parts:
            hi = lk.astype(BF16)
            split += [hi, (lk - hi.astype(F32)).astype(BF16)]
        suffix = _mm(jnp.concatenate(split, axis=0), tri)
        yield

        res = run = None
        for i, (z, sp, lk, g) in enumerate(parts):
            e = z - sp + suffix[2 * i * nq:(2 * i + 1) * nq] + suffix[(2 * i + 1) * nq:(2 * i + 2) * nq]
            if run is not None:
                e = e + run
            a = jnp.exp(e)
            if g is None:
                a = jnp.where(valid, a, 0.0)
                v_rows = newv[...].astype(BF16)
            else:
                v_rows = tile(vbuf, g)
            term = _mm(a.astype(BF16), v_rows)
            tot = jnp.sum(lk, axis=1, keepdims=True)
            res = term if res is None else res + term
            run = tot if run is None else run + tot
            if i == len(parts) // 2:
                yield

        out = jnp.where(own, res, 0.0)
        acc = out[:, 0:SB_DH]
        for h in range(1, H):
            acc = acc + out[:, h * SB_DH:(h + 1) * SB_DH]
        so_ref[j] = acc.astype(BF16)

        if j + 2 < per_step:
            fetch(elem + 2, slot)
        else:
            @pl.when(step + 1 < n_steps)
            def _():
                fetch(elem + 2, slot)

    pending = [sample_sequence(j) for j in range(per_step)]

    def advance():
        while pending:
            try:
                next(pending[0])
                return
            except StopIteration:
                pending.pop(0)

    def interleave(i, n_iter):
        advance()
        if i == n_iter - 1:
            while pending:
                advance()

    _sb_prompt_body(bias_ref, q_ref, k_ref, v_ref, o_ref, kb_ref, vb_ref, interleave)


def _sb(page_table, bias, sbq, sbk, sbv, q16, k_new, v_new, cache_k, cache_v, batch, seq, layer, steps):
    dec_batch, n_pages = page_table.shape
    H = SB_HEADS
    nq = steps * H
    per_step = dec_batch // (batch * H)
    assert per_step * batch * H == dec_batch
    depth, n_phys, page_size = cache_k.shape[:3]
    cache_k = cache_k.reshape(depth, n_phys, page_size * H, SB_DH)
    cache_v = cache_v.reshape(depth, n_phys, page_size * H, SB_DH)
    past_rows = n_pages * page_size * H
    spec = pl.BlockSpec((seq, SB_DH), lambda b, h, pt: (b, h))
    kv_spec = pl.BlockSpec((1, seq * H, SB_DH), lambda b, h, pt: (layer, b, 0))
    row = pl.BlockSpec((per_step, nq, SB_DH), lambda b, h, pt: (b * H + h, 0, 0))
    new = pl.BlockSpec((per_step, steps, SB_W), lambda b, h, pt: (b * H + h, 0, 0))
    return pl.pallas_call(
        functools.partial(_sb_kernel, layer=layer, n_pages=n_pages, steps=steps),
        grid_spec=pltpu.PrefetchScalarGridSpec(
            num_scalar_prefetch=1,
            grid=(batch, H),
            in_specs=[
                pl.BlockSpec(memory_space=pltpu.SMEM),
                spec, kv_spec, kv_spec,
                row, new, new,
                pl.BlockSpec(memory_space=pl.ANY),
                pl.BlockSpec(memory_space=pl.ANY),
            ],
            out_specs=[spec, row],
            scratch_shapes=[
                pltpu.VMEM((seq, SB_DH), BF16), pltpu.VMEM((seq, SB_DH), BF16),
                pltpu.VMEM((2, past_rows, SB_DH), F32),
                pltpu.VMEM((2, past_rows, SB_DH), F32),
                pltpu.VMEM((SB_TILE, SB_W), F32),
                pltpu.VMEM((SB_TILE, SB_W), F32),
                pltpu.SemaphoreType.DMA((2, 2)),
            ],
        ),
        out_shape=[
            jax.ShapeDtypeStruct((batch * seq, SB_W), BF16),
            jax.ShapeDtypeStruct((dec_batch, nq, SB_DH), BF16),
        ],
        compiler_params=_cparams("arbitrary", "arbitrary"),
        name="sb",
    )(page_table.reshape(-1), bias, sbq, sbk, sbv, q16, k_new, v_new, cache_k, cache_v)


def _lru_kernel(x_ref, hist_ref, h0_ref, cw_ref, cb_ref, wa_ref, ba_ref, wx_ref, bx_ref, lam_ref,
                h_ref, hl_ref, tail, hc, a_s, b_s, *xt, sub):
    token_major = bool(xt)
    taps = LRU_CONV
    if token_major:
        rows, tc, width = x_ref.shape
        xt[0][...] = jnp.swapaxes(x_ref[...], 0, 1)
        x_ref = xt[0]
    else:
        tc, rows, width = x_ref.shape

    @pl.when(pl.program_id(0) == 0)
    def _():
        tail[...] = hist_ref[...]
        hc[...] = h0_ref[...]

    lam = lam_ref[...]
    log_lam = -(jnp.maximum(-lam, 0.0) + jnp.log1p(jnp.exp(-jnp.abs(lam))))
    for j in range(tc // sub):
        t0 = j * sub
        if t0 < taps - 1:
            xe = jnp.concatenate([tail[t0:taps - 1], x_ref[0:t0 + sub]], axis=0)
        else:
            xe = x_ref[t0 - (taps - 1):t0 + sub]
        xc = cb_ref[...] + xe[0:sub] * cw_ref[0]
        for i in range(1, taps):
            xc = xc + xe[i:i + sub] * cw_ref[i]
        xc = xc.reshape(sub * rows, width)
        r_parts, i_parts = [], []
        for n in range(LRU_BLOCKS):
            blk = xc[:, n * LRU_BW:(n + 1) * LRU_BW].astype(BF16)
            r_parts.append(_mm(blk, wa_ref[n]))
            i_parts.append(_mm(blk, wx_ref[n]))
        r = 0.5 * jnp.tanh(0.5 * (jnp.concatenate(r_parts, axis=1) + ba_ref[...])) + 0.5
        gate_i = 0.5 * jnp.tanh(0.5 * (jnp.concatenate(i_parts, axis=1) + bx_ref[...])) + 0.5
        log_a = LRU_C * r * log_lam
        a = jnp.exp(log_a)
        a_s[t0:t0 + sub] = a.reshape(sub, rows, width)
        one_minus_a2 = -jnp.tanh(log_a) * (a * a + 1.0)
        b_s[t0:t0 + sub] = (jnp.sqrt(one_minus_a2) * (gate_i * xc)).reshape(sub, rows, width)
    tail[...] = x_ref[tc - (taps - 1):tc]

    hist_dst = a_s if token_major else h_ref

    def step(t, h):
        h = a_s[t] * h + b_s[t]
        hist_dst[t] = h
        return h

    h = lax.fori_loop(0, tc, step, hc[...], unroll=min(tc, 8))
    hc[...] = h
    hl_ref[...] = h
    if token_major:
        h_ref[...] = jnp.swapaxes(a_s[...], 0, 1).astype(h_ref.dtype)


def _lru(x, hist, h0, cw, cb, wa, ba, wx, bx, lam, tc, token_major):
    if token_major:
        rows, steps, width = x.shape
        blk = pl.BlockSpec((rows, tc, width), lambda i: (0, i, 0))
        h_shape = jax.ShapeDtypeStruct((rows, steps, width), BF16)
    else:
        steps, rows, width = x.shape
        blk = pl.BlockSpec((tc, rows, width), lambda i: (i, 0, 0))
        h_shape = jax.ShapeDtypeStruct((steps, rows, width), F32)
    sub = max(1, LRU_ROW_CHUNK // rows)
    vec = _resident((1, width))
    time_major_buf = pltpu.VMEM((tc, rows, width), F32)
    return pl.pallas_call(
        functools.partial(_lru_kernel, sub=sub),
        grid=(steps // tc,),
        in_specs=[
            blk, _resident((LRU_CONV - 1, rows, width)), _resident((rows, width)),
            _resident((LRU_CONV, 1, width)), vec,
            _resident((LRU_BLOCKS, LRU_BW, LRU_BW)), vec,
            _resident((LRU_BLOCKS, LRU_BW, LRU_BW)), vec, vec,
        ],
        out_specs=[blk, pl.BlockSpec((rows, width), lambda i: (0, 0))],
        out_shape=[h_shape, jax.ShapeDtypeStruct((rows, width), F32)],
        scratch_shapes=[
            pltpu.VMEM((LRU_CONV - 1, rows, width), F32),
            pltpu.VMEM((rows, width), F32),
            time_major_buf, time_major_buf,
        ] + ([time_major_buf] if token_major else []),
        compiler_params=_cparams("arbitrary"),
        name="lru",
    )(x, hist, h0, cw, cb, wa, ba, wx, bx, lam)


def _merge_kernel(x_ref, oret_ref, osb_ref, olru_ref, g_ref, wgl_ref, wr_ref, ws_ref, wl_ref, wo_ref,
                  out_ref, m_ref):
    x = x_ref[...]
    xn = _rms(x, g_ref[...]).astype(BF16)
    branches = ((oret_ref, wr_ref), (osb_ref, ws_ref), (olru_ref, wl_ref))
    cw = 256
    for c in range(D_MODEL // cw):
        cols = slice(c * cw, (c + 1) * cw)
        m = None
        for j, (o_ref, w_ref) in enumerate(branches):
            lo = PROJ_COLS + j * D_MODEL + c * cw
            gate = jax.nn.sigmoid(_mm(xn, wgl_ref[0, :, lo:lo + cw]))
            term = gate * _mm(o_ref[...], w_ref[0, :, cols])
            m = term if m is None else m + term
        m_ref[:, cols] = m.astype(BF16)
    out_ref[...] = x + _mm(m_ref[...], wo_ref[0])


def _merge(x, o_ret, o_sb, o_lru, g, w_in, wr, ws, wl, wo, layer):
    n = x.shape[0]
    tm = TOKEN_TILE
    row = lambda width: pl.BlockSpec((tm, width), lambda i: (i, 0))
    return pl.pallas_call(
        _merge_kernel,
        grid=(n // tm,),
        in_specs=[
            row(D_MODEL), row(RET_W), row(SB_W), row(LRU_WIDTH),
            _resident((1, D_MODEL)), _layer_block(w_in, layer),
            _layer_block(wr, layer), _layer_block(ws, layer), _layer_block(wl, layer),
            _layer_block(wo, layer),
        ],
        out_specs=row(D_MODEL),
        out_shape=jax.ShapeDtypeStruct((n, D_MODEL), F32),
        scratch_shapes=[pltpu.VMEM((tm, D_MODEL), BF16)],
        compiler_params=_cparams("parallel"),
        name="merge",
    )(x, o_ret, o_sb, o_lru, g, w_in, wr, ws, wl, wo)


def _ffn_kernel(x_ref, g_ref, wg_ref, wu_ref, cw_ref, cb_ref, wd_ref, hist_ref, nf_ref, out_ref, st_ref,
                carry, h_ref, *, shift, tiles_per_seq, final_norm):
    tm = x_ref.shape[0]
    hist_rows = carry.shape[0]
    i = pl.program_id(0)

    @pl.when(i % tiles_per_seq == 0)
    def _():
        carry[...] = hist_ref[...]

    x = x_ref[...]
    xn = _rms(x, g_ref[...]).astype(BF16)
    S = V7X_SUBLANES
    n_chunks = D_FF // FFN_COL_CHUNK
    col_slice = lambda c: slice(c * FFN_COL_CHUNK, (c + 1) * FFN_COL_CHUNK)

    def up(c):
        return _mm(xn, wg_ref[0, :, col_slice(c)]), _mm(xn, wu_ref[0, :, col_slice(c)])

    def act(c, g, u):
        cols = col_slice(c)
        hist = carry[:, cols]
        if shift == 1:
            ext = jnp.concatenate([hist, g[0:S]], axis=0)
            p1 = jnp.concatenate([pltpu.roll(ext, 1, 0)[S:2 * S], pltpu.roll(g, 1, 0)[S:]], axis=0)
            p2 = jnp.concatenate([pltpu.roll(ext, 2, 0)[S:2 * S], pltpu.roll(g, 2, 0)[S:]], axis=0)
        else:
            p1 = jnp.concatenate([hist[shift:2 * shift], g[0:tm - shift]], axis=0)
            p2 = jnp.concatenate([hist[0:2 * shift], g[0:tm - 2 * shift]], axis=0)
        carry[:, cols] = g[tm - hist_rows:tm]
        gc = cb_ref[:, cols] + p2 * cw_ref[0, :, cols] + p1 * cw_ref[1, :, cols] + g * cw_ref[2, :, cols]
        return (jax.nn.gelu(gc) * u).astype(BF16)

    ups = {}
    for c in range(n_chunks + 1):
        if c < n_chunks:
            ups[c] = up(c)
        if c >= 1:
            h_ref[:, col_slice(c - 1)] = act(c - 1, *ups.pop(c - 1))
    st_ref[...] = carry[hist_rows - 2 * shift:hist_rows, :].reshape(st_ref.shape)
    y = x + _mm(h_ref[...], wd_ref[0])
    if final_norm:
        y = _rms(y, nf_ref[...])
    out_ref[...] = y


def _ffn(x, g, wg, wu, cw, cb, wd, hist, nf, layer, shift, tiles_per_seq, final_norm):
    n = x.shape[0]
    tm = TOKEN_TILE
    n_seq = n // (tm * tiles_per_seq)
    hist_rows = hist.shape[0]
    row = pl.BlockSpec((tm, D_MODEL), lambda i: (i, 0))
    return pl.pallas_call(
        functools.partial(_ffn_kernel, shift=shift, tiles_per_seq=tiles_per_seq, final_norm=final_norm),
        grid=(n // tm,),
        in_specs=[
            row, _resident((1, D_MODEL)),
            _layer_block(wg, layer), _layer_block(wu, layer),
            _resident((FFN_CONV, 1, D_FF)), _resident((1, D_FF)),
            _layer_block(wd, layer),
            _resident((hist_rows, D_FF)), _resident((1, D_MODEL)),
        ],
        out_specs=[row, pl.BlockSpec((1, 2 * shift, D_FF), lambda i: (i // tiles_per_seq, 0, 0))],
        out_shape=[
            jax.ShapeDtypeStruct((n, D_MODEL), F32),
            jax.ShapeDtypeStruct((n_seq, 2 * shift, D_FF), F32),
        ],
        scratch_shapes=[pltpu.VMEM((hist_rows, D_FF), F32), pltpu.VMEM((tm, D_FF), BF16)],
        compiler_params=_cparams("arbitrary"),
        name="ffn",
    )(x, g, wg, wu, cw, cb, wd, hist, nf)


def _rot_tables(pos):
    half = RET_DK // 2
    inv = ROPE_BASE ** (-jnp.arange(half, dtype=F32) / half)
    ang = pos[:, None] * inv[None, :]
    c, s = jnp.cos(ang), jnp.sin(ang)
    return jnp.concatenate([c, c], axis=-1), jnp.concatenate([-s, s], axis=-1)


def _layer_weights(l, p):
    return dict(
        layer=l,
        norm1=p["norm1"][l][None, :],
        w_in=p["w_in"],
        ret_gn=p["ret_gn"][l][None, :],
        sb_bias=p["sb_bias"][l],
        lru_cw=p["lru_conv_w"][l][:, None, :],
        lru_cb=p["lru_conv_b"][l][None, :],
        lru_wa=p["lru_w_a"][l].astype(BF16),
        lru_ba=p["lru_b_a"][l][None, :],
        lru_wx=p["lru_w_x"][l].astype(BF16),
        lru_bx=p["lru_b_x"][l][None, :],
        lru_lam=p["lru_lambda"][l][None, :],
        w_br_ret=p["w_br_ret"], w_br_sb=p["w_br_sb"], w_br_lru=p["w_br_lru"], w_out=p["w_out"],
        norm2=p["norm2"][l][None, :],
        w_ffn_gate=p["w_ffn_gate"], w_ffn_up=p["w_ffn_up"],
        ffn_cw=p["ffn_conv_w"][l][:, None, :],
        ffn_cb=p["ffn_conv_b"][l][None, :],
        w_ffn_down=p["w_ffn_down"],
        norm_f=p["norm_f"][None, :],
    )


def _mix_and_ffn(x, o_ret, o_sb, o_lru, lw, ffn_hist, shift, tiles_per_seq, final_norm):
    x1 = _merge(x, o_ret, o_sb, o_lru, lw["norm1"], lw["w_in"], lw["w_br_ret"], lw["w_br_sb"],
                lw["w_br_lru"], lw["w_out"], lw["layer"])
    return _ffn(x1, lw["norm2"], lw["w_ffn_gate"], lw["w_ffn_up"], lw["ffn_cw"], lw["ffn_cb"],
                lw["w_ffn_down"], ffn_hist, lw["norm_f"], lw["layer"], shift, tiles_per_seq, final_norm)


def _layer(xp, xs, lw, lg, tables, batch, seq, dec_batch, steps, layer, depth, kv_stack, page_table, cache_k,
           cache_v, state_ret, ret_out, state_lru_h, state_lru_conv, state_ffn_conv, final_norm):
    (cos_p, sin_p), (cos_s, sin_s) = tables
    heads = lambda a: a.reshape(steps, dec_batch, SB_HEADS, SB_DH).transpose(1, 0, 2, 3)
    p_proj = _proj(xp, lw["norm1"], lw["w_in"], layer, cos_p, sin_p, stacked=(depth,) + kv_stack)
    s_proj = _proj(xs, lw["norm1"], lw["w_in"], layer, cos_s, sin_s)
    sbq_p, sbk_p, sbv_p = p_proj[2:5]
    sbq_s, sbk_s, sbv_s = s_proj[2:5]
    k_new, v_new = heads(sbk_s), heads(sbv_s)
    o_sb_p, o16 = _sb(page_table, lw["sb_bias"], sbq_p, sbk_p, sbv_p,
                      heads(sbq_s).reshape(dec_batch, steps * SB_HEADS, SB_DH),
                      k_new.reshape(dec_batch, steps, SB_W), v_new.reshape(dec_batch, steps, SB_W),
                      cache_k, cache_v, batch, seq, layer, steps)
    xp, st_p = _prompt_rest(xp, p_proj, o_sb_p, lw, lg, batch, seq, final_norm)
    xs, ret_out, st_s = _sample_rest(xs, s_proj, o16, k_new, v_new, lw, lg, dec_batch, steps, layer, state_ret,
                                     ret_out, state_lru_h, state_lru_conv, state_ffn_conv, final_norm)
    return xp, xs, (sbk_p, sbv_p), ret_out, st_p, st_s


def _prompt_rest(x, proj, o_sb, lw, lg, batch, seq, final_norm):
    retq, ret, _, _, _, lx = proj
    o_ret, s_new = _ret_prompt(retq, ret, lg, lw["ret_gn"], batch, seq)
    lx3 = lx.reshape(batch, seq, LRU_WIDTH)
    h, h_last = _lru(
        lx3, jnp.zeros((LRU_CONV - 1, batch, LRU_WIDTH), F32), jnp.zeros((batch, LRU_WIDTH), F32),
        lw["lru_cw"], lw["lru_cb"], lw["lru_wa"], lw["lru_ba"], lw["lru_wx"], lw["lru_bx"], lw["lru_lam"],
        tc=256, token_major=True)
    o_lru = h.reshape(batch * seq, LRU_WIDTH)
    x2, ffn_state = _mix_and_ffn(x, o_ret, o_sb, o_lru, lw, jnp.zeros((V7X_SUBLANES, D_FF), F32),
                                 shift=1, tiles_per_seq=seq // TOKEN_TILE, final_norm=final_norm)
    states = (s_new, h_last, lx3[:, seq - (LRU_CONV - 1):, :], ffn_state)
    return x2, states


def _sample_rest(x, proj, o16, k_new, v_new, lw, lg, batch, steps, layer, state_ret, ret_out, state_lru_h,
                 state_lru_conv, state_ffn_conv, final_norm):
    to_tok = lambda a: a.reshape(steps, batch, -1).transpose(1, 0, 2).reshape(batch * steps, -1)
    to_tm = lambda a: a.reshape(batch, steps, -1).transpose(1, 0, 2).reshape(steps * batch, -1)
    retq, ret, _, _, _, lx = proj
    o_ret_tok, ret_out = _ret_sample(to_tok(retq), to_tok(ret), lg, lw["ret_gn"], state_ret, steps, layer,
                                     ret_out)
    o_ret = to_tm(o_ret_tok)
    o_sb = to_tm(o16.reshape(batch * steps, SB_W))
    lx3 = lx.reshape(steps, batch, LRU_WIDTH)
    h_tm, h_last = _lru(
        lx3, state_lru_conv.transpose(1, 0, 2), state_lru_h,
        lw["lru_cw"], lw["lru_cb"], lw["lru_wa"], lw["lru_ba"], lw["lru_wx"], lw["lru_bx"], lw["lru_lam"],
        tc=steps, token_major=False)
    o_lru = h_tm.reshape(steps * batch, LRU_WIDTH).astype(BF16)
    ffn_hist = state_ffn_conv.transpose(1, 0, 2).reshape((FFN_CONV - 1) * batch, D_FF)
    x2, ffn_state = _mix_and_ffn(x, o_ret, o_sb, o_lru, lw, ffn_hist, shift=batch, tiles_per_seq=1,
                                 final_norm=final_norm)
    states = (
        k_new, v_new, h_last,
        lx3[steps - (LRU_CONV - 1):].transpose(1, 0, 2),
        ffn_state.reshape(FFN_CONV - 1, batch, D_FF).transpose(1, 0, 2),
    )
    return x2, ret_out, states


def kernel(x_prompt, x_sample, cache_sb_k, cache_sb_v, state_ret, state_lru_h, state_lru_conv, state_ffn_conv,
           page_table, norm1, w_in, ret_gn, sb_bias, lru_conv_w, lru_conv_b, lru_w_a, lru_b_a, lru_w_x, lru_b_x,
           lru_lambda, w_br_ret, w_br_sb, w_br_lru, w_out, norm2, w_ffn_gate, w_ffn_up, ffn_conv_w, ffn_conv_b,
           w_ffn_down, norm_f):
    bf = lambda w: w.astype(BF16)
    params = dict(norm1=norm1, w_in=bf(w_in), ret_gn=ret_gn, sb_bias=sb_bias, lru_conv_w=lru_conv_w,
                  lru_conv_b=lru_conv_b, lru_w_a=lru_w_a, lru_b_a=lru_b_a, lru_w_x=lru_w_x, lru_b_x=lru_b_x,
                  lru_lambda=lru_lambda, w_br_ret=bf(w_br_ret), w_br_sb=bf(w_br_sb), w_br_lru=bf(w_br_lru),
                  w_out=bf(w_out), norm2=norm2, w_ffn_gate=bf(w_ffn_gate), w_ffn_up=bf(w_ffn_up),
                  ffn_conv_w=ffn_conv_w, ffn_conv_b=ffn_conv_b, w_ffn_down=bf(w_ffn_down), norm_f=norm_f)
    depth = w_in.shape[0]
    batch, seq, _ = x_prompt.shape
    dec_batch, steps, _ = x_sample.shape
    past_len = page_table.shape[1] * cache_sb_k.shape[2]
    assert seq % TOKEN_TILE == 0 and dec_batch * steps == TOKEN_TILE and steps & (steps - 1) == 0
    assert steps >= LRU_CONV - 1 and steps >= FFN_CONV - 1

    lg = jnp.log(1.0 - 2.0 ** (-5.0 - jnp.arange(RET_HEADS, dtype=F32)))
    cos_p, sin_p = _rot_tables(jnp.arange(seq, dtype=F32))
    cos_s, sin_s = _rot_tables(past_len + jnp.repeat(jnp.arange(steps, dtype=F32), dec_batch))

    xp = x_prompt.reshape(batch * seq, D_MODEL)
    xs = x_sample.transpose(1, 0, 2).reshape(steps * dec_batch, D_MODEL)
    st_p, st_s = [], []
    kv_stack = (None, None)
    ret_sample = None
    for l in range(depth):
        lw = _layer_weights(l, params)
        last = l == depth - 1
        xp, xs, kv_stack, ret_sample, st_prompt, st_sample = _layer(
            xp, xs, lw, lg, ((cos_p, sin_p), (cos_s, sin_s)), batch, seq, dec_batch, steps, l, depth, kv_stack,
            page_table, cache_sb_k, cache_sb_v, state_ret, ret_sample, state_lru_h[l], state_lru_conv[l],
            state_ffn_conv[l], last)
        st_p.append(st_prompt)
        st_s.append(st_sample)
    y_prompt = xp.reshape(batch, seq, D_MODEL)
    y_sample = xs.reshape(steps, dec_batch, D_MODEL).transpose(1, 0, 2)
    k_prompt, v_prompt = (a.reshape(depth, batch, seq, SB_HEADS, SB_DH) for a in kv_stack)
    stack = lambda sts, j: jnp.stack([s[j] for s in sts], axis=0)
    return (y_prompt, y_sample,
            k_prompt, v_prompt, stack(st_s, 0), stack(st_s, 1),
            stack(st_p, 0), ret_sample,
            stack(st_p, 1), stack(st_s, 2),
            stack(st_p, 2), stack(st_s, 3),
            stack(st_p, 3), stack(st_s, 4))
```

```python
import functools

import jax
import jax.numpy as jnp
from jax import lax
from jax.experimental import pallas as pl
from jax.experimental.pallas import tpu as pltpu

F32 = jnp.float32
BF16 = jnp.bfloat16

D_MODEL = 1024
RET_HEADS = 4
RET_DK = 128
RET_DV = 128
RET_CHUNK = 128
ROPE_BASE = 10000.0
SB_HEADS = 4
SB_DH = 128
SB_TILE = 256
SB_Q_ROWS = 256
LRU_WIDTH = 512
LRU_BLOCKS = 4
LRU_BW = LRU_WIDTH // LRU_BLOCKS
LRU_CONV = 4
LRU_C = 8.0
N_BRANCH = 3
D_FF = 2816
FFN_CONV = 3
EPS = 1e-6

RET_W = RET_HEADS * RET_DK
SB_W = SB_HEADS * SB_DH
PROJ_COLS = 4 * RET_W + 3 * SB_W + LRU_WIDTH

V7X_LANES = 128
V7X_SUBLANES = 8
V7X_VMEM_LIMIT_BYTES = 60000 * 1024

TOKEN_TILE = 512
FFN_COL_CHUNK = 256
LRU_ROW_CHUNK = 256


def _cparams(*sem):
    return pltpu.CompilerParams(dimension_semantics=sem, vmem_limit_bytes=V7X_VMEM_LIMIT_BYTES)


def _resident(shape):
    zeros = (0,) * len(shape)
    return pl.BlockSpec(shape, lambda *_: zeros, pipeline_mode=pl.Buffered(1))


def _layer_block(w, layer):
    zeros = (0,) * (w.ndim - 1)
    return pl.BlockSpec((1,) + w.shape[1:], lambda *_: (layer,) + zeros, pipeline_mode=pl.Buffered(1))


def _mm(a, b):
    return jnp.dot(a, b, preferred_element_type=F32)


def _mm_nt(a, b):
    return lax.dot_general(a, b, (((1,), (1,)), ((), ())), preferred_element_type=F32)


def _mm_tn(a, b):
    return lax.dot_general(a, b, (((0,), (0,)), ((), ())), preferred_element_type=F32)


def _rms(x, g):
    return x * lax.rsqrt(jnp.mean(x * x, axis=-1, keepdims=True) + EPS) * g


LOG2E = 1.4426950408889634


def _neg_abs(x):
    bits = lax.bitcast_convert_type(x, jnp.uint32) | jnp.uint32(0x80000000)
    return lax.bitcast_convert_type(bits, F32)


def _softplus(z):
    return jnp.maximum(z, 0.0) + jnp.log(1.0 + jnp.exp(-jnp.abs(z)))


def _suffix_sum(lk, tri):
    hi = lk.astype(BF16)
    lo = (lk - hi.astype(F32)).astype(BF16)
    if tri.shape[0] == 2 * lk.shape[1]:
        return _mm(jnp.concatenate([hi, lo], axis=1), tri)
    return _mm(hi, tri) + _mm(lo, tri)


def _strict_lower(n):
    r = lax.broadcasted_iota(jnp.int32, (n, n), 0)
    c = lax.broadcasted_iota(jnp.int32, (n, n), 1)
    return (r > c).astype(BF16)


def _proj_kernel(x_ref, g_ref, w_ref, cos_ref, sin_ref, retq_ref, ret_ref, sbq_ref, sbk_ref, sbv_ref, lx_ref):
    w = lambda lo, width: w_ref[0, :, lo:lo + width]
    xn = _rms(x_ref[...], g_ref[...]).astype(BF16)
    cs, sn = cos_ref[...], sin_ref[...]
    rq = _mm(xn, w(0, RET_W))
    rk = _mm(xn, w(RET_W, RET_W))
    for h in range(RET_HEADS):
        sl = slice(h * RET_DK, (h + 1) * RET_DK)
        retq_ref[:, sl] = _rotary(rq[:, sl], cs, sn).astype(BF16)
        ret_ref[:, sl] = _rotary(rk[:, sl], cs, sn) * (RET_DK ** -0.5)
    for c in range(2, 4):
        ret_ref[:, (c - 1) * RET_W:c * RET_W] = _mm(xn, w(c * RET_W, RET_W))
    o = 4 * RET_W
    sbq_ref[...] = _mm(xn, w(o, SB_W)).astype(BF16)
    lx_ref[...] = _mm(xn, w(o + 3 * SB_W, LRU_WIDTH))
    for ref, lo in ((sbk_ref, o + SB_W), (sbv_ref, o + 2 * SB_W)):
        kv = _mm(xn, w(lo, SB_W))
        if len(ref.shape) == 2:
            ref[...] = kv
        else:
            tm = kv.shape[0]
            for h in range(SB_HEADS):
                ref[0, pl.ds(h, tm, stride=SB_HEADS), :] = kv[:, h * SB_DH:(h + 1) * SB_DH]
            for later in range(1, ref.shape[0]):
                ref[later] = jnp.zeros(ref.shape[1:], F32)


def _proj_kernel_stacked(x_ref, g_ref, w_ref, cos_ref, sin_ref, prev_k, prev_v, *out_refs):
    _proj_kernel(x_ref, g_ref, w_ref, cos_ref, sin_ref, *out_refs)


def _proj(x, g, w_in, layer, cos, sin, stacked=None):
    n = x.shape[0]
    tm = TOKEN_TILE
    row = lambda width: pl.BlockSpec((tm, width), lambda i: (i, 0))
    period = cos.shape[0] // tm
    table = pl.BlockSpec((tm, RET_DK), lambda i: (i % period, 0))
    in_specs = [row(D_MODEL), _resident((1, D_MODEL)), _layer_block(w_in, layer), table, table]
    args = [x, g, w_in, cos, sin]
    body, aliases = _proj_kernel, {}
    if stacked is None:
        kv_spec, kv_shape = row(SB_W), jax.ShapeDtypeStruct((n, SB_W), F32)
    else:
        depth, prev_k, prev_v = stacked
        kv_shape = jax.ShapeDtypeStruct((depth, n * SB_HEADS, SB_DH), F32)
        if prev_k is None:
            assert layer == 0
            kv_spec = pl.BlockSpec((depth, tm * SB_HEADS, SB_DH), lambda i: (0, i, 0))
        else:
            kv_spec = pl.BlockSpec((1, tm * SB_HEADS, SB_DH), lambda i: (layer, i, 0))
            body, aliases = _proj_kernel_stacked, {5: 3, 6: 4}
            in_specs += [pl.BlockSpec(memory_space=pl.ANY)] * 2
            args += [prev_k, prev_v]
    return pl.pallas_call(
        body,
        grid=(n // tm,),
        in_specs=in_specs,
        out_specs=[row(RET_W), row(3 * RET_W), row(SB_W), kv_spec, kv_spec, row(LRU_WIDTH)],
        out_shape=[
            jax.ShapeDtypeStruct((n, RET_W), BF16),
            jax.ShapeDtypeStruct((n, 3 * RET_W), F32),
            jax.ShapeDtypeStruct((n, SB_W), BF16),
            kv_shape, kv_shape,
            jax.ShapeDtypeStruct((n, LRU_WIDTH), F32),
        ],
        input_output_aliases=aliases,
        compiler_params=_cparams("parallel"),
        name="proj",
    )(*args)


def _rotary(x, cs, sn):
    return x * cs + pltpu.roll(x, RET_DK // 2, 1) * sn


def _group_norm_gate(o, gn, gate):
    mu = jnp.mean(o, axis=-1, keepdims=True)
    d = o - mu
    var = jnp.mean(d * d, axis=-1, keepdims=True)
    return d * lax.rsqrt(var + EPS) * gn * (gate * jax.nn.sigmoid(gate))


def _ret_prompt_kernel(lg_ref, q_ref, k_ref, v_ref, g_ref, gn_ref, o_ref, s_ref):
    lg = lg_ref[pl.program_id(1)]
    L = RET_CHUNK
    n_chunks = q_ref.shape[0] // L
    row = lax.broadcasted_iota(jnp.int32, (L, L), 0).astype(F32)
    col = lax.broadcasted_iota(jnp.int32, (L, L), 1).astype(F32)
    diff = row - col
    dmat = jnp.where(diff >= 0, jnp.exp(jnp.maximum(diff, 0.0) * lg), 0.0)
    q_dec = jnp.exp((row + 1.0) * lg)
    k_dec = jnp.exp((L - 1.0 - row) * lg)
    s_dec = jnp.exp(jnp.full((RET_DK, RET_DV), float(L), F32) * lg)
    gn = gn_ref[...]

    rows_of = lambda c: slice(c * L, (c + 1) * L)

    def local(c):
        rows = rows_of(c)
        qb, k = q_ref[rows, :], k_ref[rows, :]
        vb = v_ref[rows, :].astype(BF16)
        scores = (_mm_nt(qb, k.astype(BF16)) * dmat).astype(BF16)
        return qb, vb, scores, _mm_tn((k * k_dec).astype(BF16), vb)

    s = jnp.zeros((RET_DK, RET_DV), F32)

    def attend(qb, vb, scores, upd):
        nonlocal s
        o = _mm(scores, vb) + _mm(qb, s.astype(BF16)) * q_dec
        s = s_dec * s + upd
        return o

    def emit(c, o):
        rows = rows_of(c)
        o_ref[rows, :] = _group_norm_gate(o, gn, g_ref[rows, :]).astype(BF16)

    locs, outs = {}, {}
    for c in range(n_chunks + 2):
        if c < n_chunks:
            locs[c] = local(c)
        if 1 <= c <= n_chunks:
            outs[c - 1] = attend(*locs.pop(c - 1))
        if c >= 2:
            emit(c - 2, outs.pop(c - 2))
    s_ref[0, 0] = s


def _ret_prompt(retq, ret, lg, gn, batch, seq):
    H = RET_HEADS
    col = lambda j: pl.BlockSpec((seq, RET_DK), lambda b, h, j=j: (b, j * H + h))
    return pl.pallas_call(
        _ret_prompt_kernel,
        grid=(batch, H),
        in_specs=[
            pl.BlockSpec(memory_space=pltpu.SMEM),
            col(0), col(0), col(1), col(2),
            pl.BlockSpec((1, RET_DV), lambda b, h: (0, h)),
        ],
        out_specs=[
            pl.BlockSpec((seq, RET_DV), lambda b, h: (b, h)),
            pl.BlockSpec((1, 1, RET_DK, RET_DV), lambda b, h: (b, h, 0, 0)),
        ],
        out_shape=[
            jax.ShapeDtypeStruct((batch * seq, RET_W), BF16),
            jax.ShapeDtypeStruct((batch, H, RET_DK, RET_DV), F32),
        ],
        compiler_params=_cparams("parallel", "parallel"),
        name="ret_prompt",
    )(lg, retq, ret, ret, ret, gn)


def _ret_sample_kernel(lg_ref, q_ref, k_ref, v_ref, g_ref, gn_ref, st_ref, o_ref, so_ref, *, nb, steps):
    R = nb * steps
    P = RET_DK
    shift = steps.bit_length() - 1
    ri = lax.broadcasted_iota(jnp.int32, (R, P), 0)
    ci = lax.broadcasted_iota(jnp.int32, (R, P), 1)
    rb, rt = ri >> shift, ri & (steps - 1)
    cb, ct = ci >> shift, ci & (steps - 1)
    same = (rb == cb) & (ci < R) & (ct <= rt)
    dt = jnp.maximum(rt - ct, 0).astype(F32)
    t_row = rt.astype(F32)
    wide_r = lax.broadcasted_iota(jnp.int32, (R, nb * RET_DV), 0) >> shift
    wide_c = lax.broadcasted_iota(jnp.int32, (R, nb * RET_DV), 1) >> (RET_DV.bit_length() - 1)
    own = wide_r == wide_c
    zpad = jnp.zeros((P - R, RET_DK), BF16)
    zpad_w = jnp.zeros((P - R, nb * RET_DV), BF16)
    for h in range(RET_HEADS):
        lg = lg_ref[h]
        sl = slice(h * RET_DK, (h + 1) * RET_DK)
        dmat = jnp.where(same, jnp.exp(dt * lg), 0.0)
        q_dec = jnp.exp((t_row + 1.0) * lg)
        k_dec = jnp.exp((steps - 1.0 - t_row) * lg)
        s_dec = jnp.exp(jnp.full((RET_DK, RET_DV), float(steps), F32) * lg)
        qb, k, v = q_ref[:, sl], k_ref[:, sl], v_ref[:, sl]
        k_pad = jnp.concatenate([k.astype(BF16), zpad], axis=0)
        v_pad = jnp.concatenate([v.astype(BF16), zpad], axis=0)
        scores = _mm_nt(qb, k_pad) * dmat
        intra = _mm(scores.astype(BF16), v_pad)
        s_cat = jnp.concatenate([st_ref[0, b, h] for b in range(nb)], axis=1)
        wide = jnp.where(own, _mm(qb, s_cat.astype(BF16)), 0.0)
        cross = wide[:, 0:RET_DV]
        for b in range(1, nb):
            cross = cross + wide[:, b * RET_DV:(b + 1) * RET_DV]
        o = intra + cross * q_dec
        o_ref[:, sl] = _group_norm_gate(o, gn_ref[:, sl], g_ref[:, sl]).astype(BF16)
        kd_pad = jnp.concatenate([(k * k_dec).astype(BF16), zpad], axis=0)
        v_wide = jnp.where(own, jnp.concatenate([v] * nb, axis=1), 0.0).astype(BF16)
        upd = _mm_tn(kd_pad, jnp.concatenate([v_wide, zpad_w], axis=0))
        for b in range(nb):
            so_ref[0, b, h] = s_dec * st_ref[0, b, h] + upd[:, b * RET_DV:(b + 1) * RET_DV]
    for later in range(1, so_ref.shape[0]):
        so_ref[later] = jnp.zeros(so_ref.shape[1:], F32)


def _ret_sample_kernel_stacked(*refs, nb, steps):
    _ret_sample_kernel(*refs[:7], *refs[8:], nb=nb, steps=steps)


def _ret_sample(retq_tok, ret_tok, lg, gn, states, steps, layer, prev_out):
    n = ret_tok.shape[0]
    batch = n // steps
    nb = 8
    R = nb * steps
    col = lambda j: pl.BlockSpec((R, RET_W), lambda i, j=j: (i, j))
    st_spec = pl.BlockSpec((1, nb, RET_HEADS, RET_DK, RET_DV), lambda i: (layer, i, 0, 0, 0))
    in_specs = [
        pl.BlockSpec(memory_space=pltpu.SMEM),
        col(0), col(0), col(1), col(2),
        _resident((1, RET_W)),
        st_spec,
    ]
    args = [lg, retq_tok, ret_tok, ret_tok, ret_tok, gn, states]
    body, aliases = _ret_sample_kernel, {}
    if prev_out is None:
        assert layer == 0
        out_spec = pl.BlockSpec((states.shape[0], nb, RET_HEADS, RET_DK, RET_DV), lambda i: (0, i, 0, 0, 0))
    else:
        out_spec = st_spec
        body, aliases = _ret_sample_kernel_stacked, {7: 1}
        in_specs.append(pl.BlockSpec(memory_space=pl.ANY))
        args.append(prev_out)
    return pl.pallas_call(
        functools.partial(body, nb=nb, steps=steps),
        grid=(batch // nb,),
        in_specs=in_specs,
        out_specs=[pl.BlockSpec((R, RET_W), lambda i: (i, 0)), out_spec],
        out_shape=[
            jax.ShapeDtypeStruct((n, RET_W), BF16),
            jax.ShapeDtypeStruct(states.shape, F32),
        ],
        input_output_aliases=aliases,
        compiler_params=_cparams("parallel"),
        name="ret_sample",
    )(*args)


def _sb_prompt_body(bias_ref, q_ref, k_ref, v_ref, o_ref, kb_ref, vb_ref, interleave):
    head = pl.program_id(1)
    bias = bias_ref[head]
    n = SB_TILE
    m = SB_Q_ROWS
    seq = q_ref.shape[0]
    scale = SB_DH ** -0.5
    kb_ref[...] = k_ref[0, pl.ds(head, seq, stride=SB_HEADS), :].astype(BF16)
    vb_ref[...] = v_ref[0, pl.ds(head, seq, stride=SB_HEADS), :].astype(BF16)
    tri = _strict_lower(n)
    tri2 = jnp.concatenate([tri, tri], axis=0)
    row = lax.broadcasted_iota(jnp.int32, (m, n), 0)
    col = lax.broadcasted_iota(jnp.int32, (m, n), 1)
    causal_at = [col < row + off for off in range(0, n, m)]

    tiles = []
    for qi in range(seq // m):
        last = qi * m // n
        tiles += [(qi, c, causal_at[qi * m % n // m] if c == last else None, c == last, c == 0)
                  for c in range(last, -1, -1)]

    scale2, bias2 = scale * LOG2E, bias * LOG2E

    def logits(qi, c, diag, first, final):
        return _mm_nt(q_ref[qi * m:(qi + 1) * m, :], kb_ref[c * n:(c + 1) * n, :]) * scale2 + bias2

    def keep(qi, c, diag, first, final, z):
        sp = jnp.maximum(z, 0.0) + jnp.log(1.0 + jnp.exp2(_neg_abs(z))) * LOG2E
        drop = sp if diag is None else jnp.where(diag, sp, 0.0)
        return z - sp, _suffix_sum(drop, tri2), jnp.sum(drop, axis=1, keepdims=True)

    acc = run = None

    def weigh(qi, c, diag, first, final, ls, suffix, tot):
        nonlocal run
        e = ls - suffix
        a = jnp.exp2(e if first else e - run)
        if diag is not None:
            a = jnp.where(diag, a, 0.0)
        run = tot if first else run + tot
        return _mm(a.astype(BF16), vb_ref[c * n:(c + 1) * n, :])

    def collect(qi, c, diag, first, final, term):
        nonlocal acc
        acc = term if first else acc + term
        if final:
            o_ref[qi * m:(qi + 1) * m, :] = acc.astype(BF16)

    n_tiles = len(tiles)
    zs, ks, ws = {}, {}, {}
    for i in range(n_tiles + 3):
        if i < n_tiles:
            zs[i] = logits(*tiles[i])
        if 1 <= i <= n_tiles:
            ks[i - 1] = keep(*tiles[i - 1], zs.pop(i - 1))
        if 2 <= i <= n_tiles + 1:
            ws[i - 2] = weigh(*tiles[i - 2], *ks.pop(i - 2))
        if i >= 3:
            collect(*tiles[i - 3], ws.pop(i - 3))
        interleave(i, n_tiles + 3)


def _sb_kernel(pt_ref, bias_ref, q_ref, k_ref, v_ref, sq_ref, kn_ref, vn_ref, k_hbm, v_hbm, o_ref, so_ref,
               kb_ref, vb_ref, kbuf, vbuf, newk, newv, sem, *, layer, n_pages, steps):
    per_step = sq_ref.shape[0]
    assert per_step % 2 == 0
    step = pl.program_id(0) * pl.num_programs(1) + pl.program_id(1)
    n_steps = pl.num_programs(0) * pl.num_programs(1)
    H = SB_HEADS
    hbits = H.bit_length() - 1
    nq = steps * H
    page_rows = k_hbm.shape[2]
    n = SB_TILE
    n_chunks = n_pages * page_rows // (H * n)
    scale = SB_DH ** -0.5

    def copies(elem, slot):
        out = []
        for p in range(n_pages):
            pg = pt_ref[elem * n_pages + p]
            rows = pl.ds(p * page_rows, page_rows)
            out.append(pltpu.make_async_copy(k_hbm.at[layer, pg], kbuf.at[slot, rows, :], sem.at[0, slot]))
            out.append(pltpu.make_async_copy(v_hbm.at[layer, pg], vbuf.at[slot, rows, :], sem.at[1, slot]))
        return out

    def fetch(elem, slot):
        for i, c in enumerate(copies(elem, slot)):
            c.start(priority=i % 2)

    @pl.when(step == 0)
    def _():
        fetch(0, 0)
        fetch(1, 1)
        newk[...] = jnp.zeros(newk.shape, F32)
        newv[...] = jnp.zeros(newv.shape, F32)

    rw = lax.broadcasted_iota(jnp.int32, (nq, SB_W), 0)
    cw = lax.broadcasted_iota(jnp.int32, (nq, SB_W), 1)
    own = (rw & (H - 1)) == (cw >> (SB_DH.bit_length() - 1))
    rn = lax.broadcasted_iota(jnp.int32, (nq, n), 0)
    cn = lax.broadcasted_iota(jnp.int32, (nq, n), 1)
    rh = rn & (H - 1)
    bias = jnp.where(rh == 0, bias_ref[0], jnp.where(rh == 1, bias_ref[1],
                     jnp.where(rh == 2, bias_ref[2], bias_ref[3])))
    tri = _strict_lower(n)
    valid = cn < (rn >> hbits)

    def sample_sequence(j):
        elem = step * per_step + j
        slot = j % 2

        def tile(buf, g):
            heads = [buf[slot, pl.ds(g * n * H + h, n, stride=H), :] for h in range(H)]
            return jnp.concatenate(heads, axis=1).astype(BF16)

        wq = jnp.where(own, jnp.concatenate([sq_ref[j]] * H, axis=1), jnp.zeros((nq, SB_W), BF16))

        def scores(k_rows):
            z = _mm_nt(wq, k_rows) * scale + bias
            return z, _softplus(z)

        newk[0:steps, :] = kn_ref[j]
        newv[0:steps, :] = vn_ref[j]
        z, sp = scores(newk[...].astype(BF16))
        parts = [(z, sp, jnp.where(valid, -sp, 0.0), None)]

        for c in copies(elem, slot):
            c.wait()

        for g in range(n_chunks - 1, -1, -1):
            z, sp = scores(tile(kbuf, g))
            parts.append((z, sp, -sp, g))

        split = []
        for _, _, lk, _ in parts:
            hi = lk.astype(BF16)
            split += [hi, (lk - hi.astype(F32)).astype(BF16)]
        suffix = _mm(jnp.concatenate(split, axis=0), tri)

        res = run = None
        for i, (z, sp, lk, g) in enumerate(parts):
            e = z - sp + suffix[2 * i * nq:(2 * i + 1) * nq] + suffix[(2 * i + 1) * nq:(2 * i + 2) * nq]
            if run is not None:
                e = e + run
            a = jnp.exp(e)
            if g is None:
                a = jnp.where(valid, a, 0.0)
                v_rows = newv[...].astype(BF16)
            else:
                v_rows = tile(vbuf, g)
            term = _mm(a.astype(BF16), v_rows)
            tot = jnp.sum(lk, axis=1, keepdims=True)
            res = term if res is None else res + term
            run = tot if run is None else run + tot

        out = jnp.where(own, res, 0.0)
        acc = out[:, 0:SB_DH]
        for h in range(1, H):
            acc = acc + out[:, h * SB_DH:(h + 1) * SB_DH]
        so_ref[j] = acc.astype(BF16)

        if j + 2 < per_step:
            fetch(elem + 2, slot)
        else:
            @pl.when(step + 1 < n_steps)
            def _():
                fetch(elem + 2, slot)

    def interleave(i, n_iter):
        for j in range(per_step):
            if i == (j + 1) * n_iter // (per_step + 1):
                sample_sequence(j)

    _sb_prompt_body(bias_ref, q_ref, k_ref, v_ref, o_ref, kb_ref, vb_ref, interleave)


def _sb(page_table, bias, sbq, sbk, sbv, q16, k_new, v_new, cache_k, cache_v, batch, seq, layer, steps):
    dec_batch, n_pages = page_table.shape
    H = SB_HEADS
    nq = steps * H
    per_step = dec_batch // (batch * H)
    assert per_step * batch * H == dec_batch
    depth, n_phys, page_size = cache_k.shape[:3]
    cache_k = cache_k.reshape(depth, n_phys, page_size * H, SB_DH)
    cache_v = cache_v.reshape(depth, n_phys, page_size * H, SB_DH)
    past_rows = n_pages * page_size * H
    spec = pl.BlockSpec((seq, SB_DH), lambda b, h, pt: (b, h))
    kv_spec = pl.BlockSpec((1, seq * H, SB_DH), lambda b, h, pt: (layer, b, 0))
    row = pl.BlockSpec((per_step, nq, SB_DH), lambda b, h, pt: (b * H + h, 0, 0))
    new = pl.BlockSpec((per_step, steps, SB_W), lambda b, h, pt: (b * H + h, 0, 0))
    return pl.pallas_call(
        functools.partial(_sb_kernel, layer=layer, n_pages=n_pages, steps=steps),
        grid_spec=pltpu.PrefetchScalarGridSpec(
            num_scalar_prefetch=1,
            grid=(batch, H),
            in_specs=[
                pl.BlockSpec(memory_space=pltpu.SMEM),
                spec, kv_spec, kv_spec,
                row, new, new,
                pl.BlockSpec(memory_space=pl.ANY),
                pl.BlockSpec(memory_space=pl.ANY),
            ],
            out_specs=[spec, row],
            scratch_shapes=[
                pltpu.VMEM((seq, SB_DH), BF16), pltpu.VMEM((seq, SB_DH), BF16),
                pltpu.VMEM((2, past_rows, SB_DH), F32),
                pltpu.VMEM((2, past_rows, SB_DH), F32),
                pltpu.VMEM((SB_TILE, SB_W), F32),
                pltpu.VMEM((SB_TILE, SB_W), F32),
                pltpu.SemaphoreType.DMA((2, 2)),
            ],
        ),
        out_shape=[
            jax.ShapeDtypeStruct((batch * seq, SB_W), BF16),
            jax.ShapeDtypeStruct((dec_batch, nq, SB_DH), BF16),
        ],
        compiler_params=_cparams("arbitrary", "arbitrary"),
        name="sb",
    )(page_table.reshape(-1), bias, sbq, sbk, sbv, q16, k_new, v_new, cache_k, cache_v)


def _lru_kernel(x_ref, hist_ref, h0_ref, cw_ref, cb_ref, wa_ref, ba_ref, wx_ref, bx_ref, lam_ref,
                h_ref, hl_ref, tail, hc, a_s, b_s, *xt, sub):
    token_major = bool(xt)
    taps = LRU_CONV
    if token_major:
        rows, tc, width = x_ref.shape
        xt[0][...] = jnp.swapaxes(x_ref[...], 0, 1)
        x_ref = xt[0]
    else:
        tc, rows, width = x_ref.shape

    @pl.when(pl.program_id(0) == 0)
    def _():
        tail[...] = hist_ref[...]
        hc[...] = h0_ref[...]

    lam = lam_ref[...]
    log_lam = -(jnp.maximum(-lam, 0.0) + jnp.log1p(jnp.exp(-jnp.abs(lam))))
    for j in range(tc // sub):
        t0 = j * sub
        if t0 < taps - 1:
            xe = jnp.concatenate([tail[t0:taps - 1], x_ref[0:t0 + sub]], axis=0)
        else:
            xe = x_ref[t0 - (taps - 1):t0 + sub]
        xc = cb_ref[...] + xe[0:sub] * cw_ref[0]
        for i in range(1, taps):
            xc = xc + xe[i:i + sub] * cw_ref[i]
        xc = xc.reshape(sub * rows, width)
        r_parts, i_parts = [], []
        for n in range(LRU_BLOCKS):
            blk = xc[:, n * LRU_BW:(n + 1) * LRU_BW].astype(BF16)
            r_parts.append(_mm(blk, wa_ref[n]))
            i_parts.append(_mm(blk, wx_ref[n]))
        r = 0.5 * jnp.tanh(0.5 * (jnp.concatenate(r_parts, axis=1) + ba_ref[...])) + 0.5
        gate_i = 0.5 * jnp.tanh(0.5 * (jnp.concatenate(i_parts, axis=1) + bx_ref[...])) + 0.5
        log_a = LRU_C * r * log_lam
        a = jnp.exp(log_a)
        a_s[t0:t0 + sub] = a.reshape(sub, rows, width)
        one_minus_a2 = -jnp.tanh(log_a) * (a * a + 1.0)
        b_s[t0:t0 + sub] = (jnp.sqrt(one_minus_a2) * (gate_i * xc)).reshape(sub, rows, width)
    tail[...] = x_ref[tc - (taps - 1):tc]

    hist_dst = a_s if token_major else h_ref

    def step(t, h):
        h = a_s[t] * h + b_s[t]
        hist_dst[t] = h
        return h

    h = lax.fori_loop(0, tc, step, hc[...], unroll=min(tc, 8))
    hc[...] = h
    hl_ref[...] = h
    if token_major:
        h_ref[...] = jnp.swapaxes(a_s[...], 0, 1).astype(h_ref.dtype)


def _lru(x, hist, h0, cw, cb, wa, ba, wx, bx, lam, tc, token_major):
    if token_major:
        rows, steps, width = x.shape
        blk = pl.BlockSpec((rows, tc, width), lambda i: (0, i, 0))
        h_shape = jax.ShapeDtypeStruct((rows, steps, width), BF16)
    else:
        steps, rows, width = x.shape
        blk = pl.BlockSpec((tc, rows, width), lambda i: (i, 0, 0))
        h_shape = jax.ShapeDtypeStruct((steps, rows, width), F32)
    sub = max(1, LRU_ROW_CHUNK // rows)
    vec = _resident((1, width))
    time_major_buf = pltpu.VMEM((tc, rows, width), F32)
    return pl.pallas_call(
        functools.partial(_lru_kernel, sub=sub),
        grid=(steps // tc,),
        in_specs=[
            blk, _resident((LRU_CONV - 1, rows, width)), _resident((rows, width)),
            _resident((LRU_CONV, 1, width)), vec,
            _resident((LRU_BLOCKS, LRU_BW, LRU_BW)), vec,
            _resident((LRU_BLOCKS, LRU_BW, LRU_BW)), vec, vec,
        ],
        out_specs=[blk, pl.BlockSpec((rows, width), lambda i: (0, 0))],
        out_shape=[h_shape, jax.ShapeDtypeStruct((rows, width), F32)],
        scratch_shapes=[
            pltpu.VMEM((LRU_CONV - 1, rows, width), F32),
            pltpu.VMEM((rows, width), F32),
            time_major_buf, time_major_buf,
        ] + ([time_major_buf] if token_major else []),
        compiler_params=_cparams("arbitrary"),
        name="lru",
    )(x, hist, h0, cw, cb, wa, ba, wx, bx, lam)


def _merge_kernel(x_ref, oret_ref, osb_ref, olru_ref, g_ref, wgl_ref, wr_ref, ws_ref, wl_ref, wo_ref,
                  out_ref, m_ref):
    x = x_ref[...]
    xn = _rms(x, g_ref[...]).astype(BF16)
    branches = ((oret_ref, wr_ref), (osb_ref, ws_ref), (olru_ref, wl_ref))
    cw = 256
    for c in range(D_MODEL // cw):
        cols = slice(c * cw, (c + 1) * cw)
        m = None
        for j, (o_ref, w_ref) in enumerate(branches):
            lo = PROJ_COLS + j * D_MODEL + c * cw
            gate = jax.nn.sigmoid(_mm(xn, wgl_ref[0, :, lo:lo + cw]))
            term = gate * _mm(o_ref[...], w_ref[0, :, cols])
            m = term if m is None else m + term
        m_ref[:, cols] = m.astype(BF16)
    out_ref[...] = x + _mm(m_ref[...], wo_ref[0])


def _merge(x, o_ret, o_sb, o_lru, g, w_in, wr, ws, wl, wo, layer):
    n = x.shape[0]
    tm = TOKEN_TILE
    row = lambda width: pl.BlockSpec((tm, width), lambda i: (i, 0))
    return pl.pallas_call(
        _merge_kernel,
        grid=(n // tm,),
        in_specs=[
            row(D_MODEL), row(RET_W), row(SB_W), row(LRU_WIDTH),
            _resident((1, D_MODEL)), _layer_block(w_in, layer),
            _layer_block(wr, layer), _layer_block(ws, layer), _layer_block(wl, layer),
            _layer_block(wo, layer),
        ],
        out_specs=row(D_MODEL),
        out_shape=jax.ShapeDtypeStruct((n, D_MODEL), F32),
        scratch_shapes=[pltpu.VMEM((tm, D_MODEL), BF16)],
        compiler_params=_cparams("parallel"),
        name="merge",
    )(x, o_ret, o_sb, o_lru, g, w_in, wr, ws, wl, wo)


def _ffn_kernel(x_ref, g_ref, wg_ref, wu_ref, cw_ref, cb_ref, wd_ref, hist_ref, nf_ref, out_ref, st_ref,
                carry, h_ref, *, shift, tiles_per_seq, final_norm):
    tm = x_ref.shape[0]
    hist_rows = carry.shape[0]
    i = pl.program_id(0)

    @pl.when(i % tiles_per_seq == 0)
    def _():
        carry[...] = hist_ref[...]

    x = x_ref[...]
    xn = _rms(x, g_ref[...]).astype(BF16)
    S = V7X_SUBLANES
    n_chunks = D_FF // FFN_COL_CHUNK
    col_slice = lambda c: slice(c * FFN_COL_CHUNK, (c + 1) * FFN_COL_CHUNK)

    def up(c):
        return _mm(xn, wg_ref[0, :, col_slice(c)]), _mm(xn, wu_ref[0, :, col_slice(c)])

    def act(c, g, u):
        cols = col_slice(c)
        hist = carry[:, cols]
        if shift == 1:
            ext = jnp.concatenate([hist, g[0:S]], axis=0)
            p1 = jnp.concatenate([pltpu.roll(ext, 1, 0)[S:2 * S], pltpu.roll(g, 1, 0)[S:]], axis=0)
            p2 = jnp.concatenate([pltpu.roll(ext, 2, 0)[S:2 * S], pltpu.roll(g, 2, 0)[S:]], axis=0)
        else:
            p1 = jnp.concatenate([hist[shift:2 * shift], g[0:tm - shift]], axis=0)
            p2 = jnp.concatenate([hist[0:2 * shift], g[0:tm - 2 * shift]], axis=0)
        carry[:, cols] = g[tm - hist_rows:tm]
        gc = cb_ref[:, cols] + p2 * cw_ref[0, :, cols] + p1 * cw_ref[1, :, cols] + g * cw_ref[2, :, cols]
        return (jax.nn.gelu(gc) * u).astype(BF16)

    ups = {}
    for c in range(n_chunks + 1):
        if c < n_chunks:
            ups[c] = up(c)
        if c >= 1:
            h_ref[:, col_slice(c - 1)] = act(c - 1, *ups.pop(c - 1))
    st_ref[...] = carry[hist_rows - 2 * shift:hist_rows, :].reshape(st_ref.shape)
    y = x + _mm(h_ref[...], wd_ref[0])
    if final_norm:
        y = _rms(y, nf_ref[...])
    out_ref[...] = y


def _ffn(x, g, wg, wu, cw, cb, wd, hist, nf, layer, shift, tiles_per_seq, final_norm):
    n = x.shape[0]
    tm = TOKEN_TILE
    n_seq = n // (tm * tiles_per_seq)
    hist_rows = hist.shape[0]
    row = pl.BlockSpec((tm, D_MODEL), lambda i: (i, 0))
    return pl.pallas_call(
        functools.partial(_ffn_kernel, shift=shift, tiles_per_seq=tiles_per_seq, final_norm=final_norm),
        grid=(n // tm,),
        in_specs=[
            row, _resident((1, D_MODEL)),
            _layer_block(wg, layer), _layer_block(wu, layer),
            _resident((FFN_CONV, 1, D_FF)), _resident((1, D_FF)),
            _layer_block(wd, layer),
            _resident((hist_rows, D_FF)), _resident((1, D_MODEL)),
        ],
        out_specs=[row, pl.BlockSpec((1, 2 * shift, D_FF), lambda i: (i // tiles_per_seq, 0, 0))],
        out_shape=[
            jax.ShapeDtypeStruct((n, D_MODEL), F32),
            jax.ShapeDtypeStruct((n_seq, 2 * shift, D_FF), F32),
        ],
        scratch_shapes=[pltpu.VMEM((hist_rows, D_FF), F32), pltpu.VMEM((tm, D_FF), BF16)],
        compiler_params=_cparams("arbitrary"),
        name="ffn",
    )(x, g, wg, wu, cw, cb, wd, hist, nf)


def _rot_tables(pos):
    half = RET_DK // 2
    inv = ROPE_BASE ** (-jnp.arange(half, dtype=F32) / half)
    ang = pos[:, None] * inv[None, :]
    c, s = jnp.cos(ang), jnp.sin(ang)
    return jnp.concatenate([c, c], axis=-1), jnp.concatenate([-s, s], axis=-1)


def _layer_weights(l, p):
    return dict(
        layer=l,
        norm1=p["norm1"][l][None, :],
        w_in=p["w_in"],
        ret_gn=p["ret_gn"][l][None, :],
        sb_bias=p["sb_bias"][l],
        lru_cw=p["lru_conv_w"][l][:, None, :],
        lru_cb=p["lru_conv_b"][l][None, :],
        lru_wa=p["lru_w_a"][l].astype(BF16),
        lru_ba=p["lru_b_a"][l][None, :],
        lru_wx=p["lru_w_x"][l].astype(BF16),
        lru_bx=p["lru_b_x"][l][None, :],
        lru_lam=p["lru_lambda"][l][None, :],
        w_br_ret=p["w_br_ret"], w_br_sb=p["w_br_sb"], w_br_lru=p["w_br_lru"], w_out=p["w_out"],
        norm2=p["norm2"][l][None, :],
        w_ffn_gate=p["w_ffn_gate"], w_ffn_up=p["w_ffn_up"],
        ffn_cw=p["ffn_conv_w"][l][:, None, :],
        ffn_cb=p["ffn_conv_b"][l][None, :],
        w_ffn_down=p["w_ffn_down"],
        norm_f=p["norm_f"][None, :],
    )


def _mix_and_ffn(x, o_ret, o_sb, o_lru, lw, ffn_hist, shift, tiles_per_seq, final_norm):
    x1 = _merge(x, o_ret, o_sb, o_lru, lw["norm1"], lw["w_in"], lw["w_br_ret"], lw["w_br_sb"],
                lw["w_br_lru"], lw["w_out"], lw["layer"])
    return _ffn(x1, lw["norm2"], lw["w_ffn_gate"], lw["w_ffn_up"], lw["ffn_cw"], lw["ffn_cb"],
                lw["w_ffn_down"], ffn_hist, lw["norm_f"], lw["layer"], shift, tiles_per_seq, final_norm)


def _layer(xp, xs, lw, lg, tables, batch, seq, dec_batch, steps, layer, depth, kv_stack, page_table, cache_k,
           cache_v, state_ret, ret_out, state_lru_h, state_lru_conv, state_ffn_conv, final_norm):
    (cos_p, sin_p), (cos_s, sin_s) = tables
    heads = lambda a: a.reshape(steps, dec_batch, SB_HEADS, SB_DH).transpose(1, 0, 2, 3)
    p_proj = _proj(xp, lw["norm1"], lw["w_in"], layer, cos_p, sin_p, stacked=(depth,) + kv_stack)
    s_proj = _proj(xs, lw["norm1"], lw["w_in"], layer, cos_s, sin_s)
    sbq_p, sbk_p, sbv_p = p_proj[2:5]
    sbq_s, sbk_s, sbv_s = s_proj[2:5]
    k_new, v_new = heads(sbk_s), heads(sbv_s)
    o_sb_p, o16 = _sb(page_table, lw["sb_bias"], sbq_p, sbk_p, sbv_p,
                      heads(sbq_s).reshape(dec_batch, steps * SB_HEADS, SB_DH),
                      k_new.reshape(dec_batch, steps, SB_W), v_new.reshape(dec_batch, steps, SB_W),
                      cache_k, cache_v, batch, seq, layer, steps)
    xp, st_p = _prompt_rest(xp, p_proj, o_sb_p, lw, lg, batch, seq, final_norm)
    xs, ret_out, st_s = _sample_rest(xs, s_proj, o16, k_new, v_new, lw, lg, dec_batch, steps, layer, state_ret,
                                     ret_out, state_lru_h, state_lru_conv, state_ffn_conv, final_norm)
    return xp, xs, (sbk_p, sbv_p), ret_out, st_p, st_s


def _prompt_rest(x, proj, o_sb, lw, lg, batch, seq, final_norm):
    retq, ret, _, _, _, lx = proj
    o_ret, s_new = _ret_prompt(retq, ret, lg, lw["ret_gn"], batch, seq)
    lx3 = lx.reshape(batch, seq, LRU_WIDTH)
    h, h_last = _lru(
        lx3, jnp.zeros((LRU_CONV - 1, batch, LRU_WIDTH), F32), jnp.zeros((batch, LRU_WIDTH), F32),
        lw["lru_cw"], lw["lru_cb"], lw["lru_wa"], lw["lru_ba"], lw["lru_wx"], lw["lru_bx"], lw["lru_lam"],
        tc=256, token_major=True)
    o_lru = h.reshape(batch * seq, LRU_WIDTH)
    x2, ffn_state = _mix_and_ffn(x, o_ret, o_sb, o_lru, lw, jnp.zeros((V7X_SUBLANES, D_FF), F32),
                                 shift=1, tiles_per_seq=seq // TOKEN_TILE, final_norm=final_norm)
    states = (s_new, h_last, lx3[:, seq - (LRU_CONV - 1):, :], ffn_state)
    return x2, states


def _sample_rest(x, proj, o16, k_new, v_new, lw, lg, batch, steps, layer, state_ret, ret_out, state_lru_h,
                 state_lru_conv, state_ffn_conv, final_norm):
    to_tok = lambda a: a.reshape(steps, batch, -1).transpose(1, 0, 2).reshape(batch * steps, -1)
    to_tm = lambda a: a.reshape(batch, steps, -1).transpose(1, 0, 2).reshape(steps * batch, -1)
    retq, ret, _, _, _, lx = proj
    o_ret_tok, ret_out = _ret_sample(to_tok(retq), to_tok(ret), lg, lw["ret_gn"], state_ret, steps, layer,
                                     ret_out)
    o_ret = to_tm(o_ret_tok)
    o_sb = to_tm(o16.reshape(batch * steps, SB_W))
    lx3 = lx.reshape(steps, batch, LRU_WIDTH)
    h_tm, h_last = _lru(
        lx3, state_lru_conv.transpose(1, 0, 2), state_lru_h,
        lw["lru_cw"], lw["lru_cb"], lw["lru_wa"], lw["lru_ba"], lw["lru_wx"], lw["lru_bx"], lw["lru_lam"],
        tc=steps, token_major=False)
    o_lru = h_tm.reshape(steps * batch, LRU_WIDTH).astype(BF16)
    ffn_hist = state_ffn_conv.transpose(1, 0, 2).reshape((FFN_CONV - 1) * batch, D_FF)
    x2, ffn_state = _mix_and_ffn(x, o_ret, o_sb, o_lru, lw, ffn_hist, shift=batch, tiles_per_seq=1,
                                 final_norm=final_norm)
    states = (
        k_new, v_new, h_last,
        lx3[steps - (LRU_CONV - 1):].transpose(1, 0, 2),
        ffn_state.reshape(FFN_CONV - 1, batch, D_FF).transpose(1, 0, 2),
    )
    return x2, ret_out, states


def kernel(x_prompt, x_sample, cache_sb_k, cache_sb_v, state_ret, state_lru_h, state_lru_conv, state_ffn_conv,
           page_table, norm1, w_in, ret_gn, sb_bias, lru_conv_w, lru_conv_b, lru_w_a, lru_b_a, lru_w_x, lru_b_x,
           lru_lambda, w_br_ret, w_br_sb, w_br_lru, w_out, norm2, w_ffn_gate, w_ffn_up, ffn_conv_w, ffn_conv_b,
           w_ffn_down, norm_f):
    bf = lambda w: w.astype(BF16)
    params = dict(norm1=norm1, w_in=bf(w_in), ret_gn=ret_gn, sb_bias=sb_bias, lru_conv_w=lru_conv_w,
                  lru_conv_b=lru_conv_b, lru_w_a=lru_w_a, lru_b_a=lru_b_a, lru_w_x=lru_w_x, lru_b_x=lru_b_x,
                  lru_lambda=lru_lambda, w_br_ret=bf(w_br_ret), w_br_sb=bf(w_br_sb), w_br_lru=bf(w_br_lru),
                  w_out=bf(w_out), norm2=norm2, w_ffn_gate=bf(w_ffn_gate), w_ffn_up=bf(w_ffn_up),
                  ffn_conv_w=ffn_conv_w, ffn_conv_b=ffn_conv_b, w_ffn_down=bf(w_ffn_down), norm_f=norm_f)
    depth = w_in.shape[0]
    batch, seq, _ = x_prompt.shape
    dec_batch, steps, _ = x_sample.shape
    past_len = page_table.shape[1] * cache_sb_k.shape[2]
    assert seq % TOKEN_TILE == 0 and dec_batch * steps == TOKEN_TILE and steps & (steps - 1) == 0
    assert steps >= LRU_CONV - 1 and steps >= FFN_CONV - 1

    lg = jnp.log(1.0 - 2.0 ** (-5.0 - jnp.arange(RET_HEADS, dtype=F32)))
    cos_p, sin_p = _rot_tables(jnp.arange(seq, dtype=F32))
    cos_s, sin_s = _rot_tables(past_len + jnp.repeat(jnp.arange(steps, dtype=F32), dec_batch))

    xp = x_prompt.reshape(batch * seq, D_MODEL)
    xs = x_sample.transpose(1, 0, 2).reshape(steps * dec_batch, D_MODEL)
    st_p, st_s = [], []
    kv_stack = (None, None)
    ret_sample = None
    for l in range(depth):
        lw = _layer_weights(l, params)
        last = l == depth - 1
        xp, xs, kv_stack, ret_sample, st_prompt, st_sample = _layer(
            xp, xs, lw, lg, ((cos_p, sin_p), (cos_s, sin_s)), batch, seq, dec_batch, steps, l, depth, kv_stack,
            page_table, cache_sb_k, cache_sb_v, state_ret, ret_sample, state_lru_h[l], state_lru_conv[l],
            state_ffn_conv[l], last)
        st_p.append(st_prompt)
        st_s.append(st_sample)
    y_prompt = xp.reshape(batch, seq, D_MODEL)
    y_sample = xs.reshape(steps, dec_batch, D_MODEL).transpose(1, 0, 2)
    k_prompt, v_prompt = (a.reshape(depth, batch, seq, SB_HEADS, SB_DH) for a in kv_stack)
    stack = lambda sts, j: jnp.stack([s[j] for s in sts], axis=0)
    return (y_prompt, y_sample,
            k_prompt, v_prompt, stack(st_s, 0), stack(st_s, 1),
            stack(st_p, 0), ret_sample,
            stack(st_p, 1), stack(st_s, 2),
            stack(st_p, 2), stack(st_s, 3),
            stack(st_p, 3), stack(st_s, 4))
```
